```python
import jax, jax.numpy as jnp
from jax import lax
import numpy as np

D_MODEL = 1024
BATCH = 8
SEQ = 4096
DEPTH = 4

HEAD_DIM = 64
HEADS_PER_GROUP = 8
DILATION_GROUPS = ((128, 1), (512, 4), (2048, 16))
N_GROUPS = len(DILATION_GROUPS)
N_ATTN_HEADS = N_GROUPS * HEADS_PER_GROUP
QKV_WIDTH = N_ATTN_HEADS * HEAD_DIM
ATTN_OUT = HEADS_PER_GROUP * HEAD_DIM
CONV_WIDTH = D_MODEL
CONV_K = 3
D_FF = ((8 * D_MODEL // 3) + 127) // 128 * 128
IN_WIDTH = 3 * QKV_WIDTH + 3 * CONV_WIDTH + 2 * D_MODEL
NUM_BUCKETS = 32
MAX_DISTANCE = 2048
BLOCK = 128
N_SUB = 3
EPS = 1e-6
NEG_INF = -1e30

kernel_name = "hybrid_macaron_conv_dilated_attn"


def _t5_bucket(dist):
    exact = NUM_BUCKETS // 2
    d = np.maximum(dist, 1).astype(np.float32)
    large = exact + (np.log(d / exact) / np.log(MAX_DISTANCE / exact) * (NUM_BUCKETS - exact)).astype(np.int32)
    large = np.minimum(large, NUM_BUCKETS - 1)
    return np.where(dist < exact, dist, large).astype(np.int32)


def _rmsnorm(x, g):
    xf = x.astype(jnp.float32)
    y = xf * lax.rsqrt(jnp.mean(xf * xf, axis=-1, keepdims=True) + EPS) * g.astype(jnp.float32)
    return y.astype(x.dtype)


def _swiglu(h, w_gate, w_up, w_down):
    return (jax.nn.silu(h @ w_gate) * (h @ w_up)) @ w_down


def _causal_dwconv(u, w):
    rhs = w.astype(u.dtype).reshape(CONV_K, 1, u.shape[-1])
    return lax.conv_general_dilated(u, rhs, window_strides=(1,), padding=[(CONV_K - 1, 0)],
                                    dimension_numbers=("NWC", "WIO", "NWC"),
                                    feature_group_count=u.shape[-1])


def _dilated_window_attention(q, k, v, bias_tab, window, dilation):
    B, S, H, E = q.shape
    span = window // dilation
    L = S // dilation
    nb = -(-L // BLOCK)
    Lp = nb * BLOCK

    def to_sub(t):
        t = t.reshape(B, L, dilation, H, E)
        return jnp.pad(t, ((0, 0), (0, Lp - L), (0, 0), (0, 0), (0, 0)))

    qs, ks, vs = to_sub(q), to_sub(k), to_sub(v)
    qb = qs.reshape(B, nb, BLOCK, dilation, H, E)

    def key_blocks(t):
        tp = jnp.pad(t, ((0, 0), (BLOCK, 0), (0, 0), (0, 0), (0, 0)))
        prev = tp[:, :Lp].reshape(B, nb, BLOCK, dilation, H, E)
        cur = t.reshape(B, nb, BLOCK, dilation, H, E)
        return jnp.concatenate([prev, cur], axis=2)

    kb, vb = key_blocks(ks), key_blocks(vs)

    i = np.arange(BLOCK)[:, None]
    j = np.arange(2 * BLOCK)[None, :]
    rel = i - j + BLOCK
    band = (rel >= 0) & (rel <= span)
    first_ok = (np.arange(nb)[:, None, None] > 0) | (j[None] >= BLOCK)
    mask = jnp.asarray(band[None] & first_ok)[None, :, None, None]
    bucket = jnp.asarray(_t5_bucket(np.maximum(rel, 0) * dilation))
    bias = jnp.transpose(bias_tab[bucket].astype(jnp.float32), (2, 0, 1))

    logits = jnp.einsum("bnqrhe,bnkrhe->bnrhqk", qb, kb).astype(jnp.float32) * (HEAD_DIM ** -0.5)
    logits = jnp.where(mask, logits + bias, NEG_INF)
    m = jnp.max(logits, axis=-1, keepdims=True)
    p = jnp.exp(logits - m)
    s = jnp.sum(p, axis=-1)
    o = jnp.einsum("bnrhqk,bnkrhe->bnqrhe", p, vb.astype(jnp.float32))
    s_q = jnp.transpose(s, (0, 1, 4, 2, 3))
    o = o / s_q[..., None]
    lse = jnp.transpose(m[..., 0] + jnp.log(s), (0, 1, 4, 2, 3))
    o = o.reshape(B, Lp, dilation, H, E)[:, :L].reshape(B, S, H, E)
    lse = lse.reshape(B, Lp, dilation, H)[:, :L].reshape(B, S, H)
    return o, lse


def _mixer(h, w_in, conv_w, w_conv_out, w_attn_out, w_o, rel_bias):
    B, S, _ = h.shape
    u = h @ w_in
    splits = np.cumsum([QKV_WIDTH, QKV_WIDTH, QKV_WIDTH, CONV_WIDTH, CONV_WIDTH, CONV_WIDTH, D_MODEL])
    q, k, v, cb, cc, ch, g_conv, g_attn = jnp.split(u, splits, axis=-1)

    y_conv = (cb * _causal_dwconv(cc * ch, conv_w)) @ w_conv_out

    q = q.reshape(B, S, N_GROUPS, HEADS_PER_GROUP, HEAD_DIM)
    k = k.reshape(B, S, N_GROUPS, HEADS_PER_GROUP, HEAD_DIM)
    v = v.reshape(B, S, N_GROUPS, HEADS_PER_GROUP, HEAD_DIM)
    outs, lses = [], []
    for g, (window, dilation) in enumerate(DILATION_GROUPS):
        tab = rel_bias[:, g * HEADS_PER_GROUP:(g + 1) * HEADS_PER_GROUP]
        o_g, lse_g = _dilated_window_attention(q[:, :, g], k[:, :, g], v[:, :, g], tab, window, dilation)
        outs.append(o_g)
        lses.append(lse_g)
    alpha = jax.nn.softmax(jnp.stack(lses, axis=0), axis=0)
    o = jnp.sum(alpha[..., None] * jnp.stack(outs, axis=0), axis=0)
    y_attn = o.reshape(B, S, ATTN_OUT).astype(h.dtype) @ w_attn_out

    merged = jax.nn.sigmoid(g_conv) * y_conv + jax.nn.sigmoid(g_attn) * y_attn
    return merged @ w_o


def setup_inputs(seed: int = 0) -> dict:
    key = jax.random.key(seed)
    ks = jax.random.split(key, 16)
    f32 = jnp.float32

    def nrm(k, shape, fan_in):
        return jax.random.normal(k, shape, f32) * (fan_in ** -0.5)

    return {
        "x": jax.random.normal(ks[0], (BATCH, SEQ, D_MODEL), f32),
        "c": jax.random.normal(ks[1], (BATCH, D_MODEL), f32),
        "ada_w": nrm(ks[2], (DEPTH, D_MODEL, N_SUB * 3 * D_MODEL), D_MODEL),
        "ada_b": 0.02 * jax.random.normal(ks[3], (DEPTH, N_SUB * 3 * D_MODEL), f32),
        "norm_g": 1.0 + 0.05 * jax.random.normal(ks[4], (DEPTH, N_SUB, D_MODEL), f32),
        "ffn_w_gate": nrm(ks[5], (DEPTH, 2, D_MODEL, D_FF), D_MODEL),
        "ffn_w_up": nrm(ks[6], (DEPTH, 2, D_MODEL, D_FF), D_MODEL),
        "ffn_w_down": nrm(ks[7], (DEPTH, 2, D_FF, D_MODEL), D_FF),
        "w_in": nrm(ks[8], (DEPTH, D_MODEL, IN_WIDTH), D_MODEL),
        "conv_w": nrm(ks[9], (DEPTH, CONV_K, CONV_WIDTH), CONV_K),
        "w_conv_out": nrm(ks[10], (DEPTH, CONV_WIDTH, D_MODEL), CONV_WIDTH),
        "w_attn_out": nrm(ks[11], (DEPTH, ATTN_OUT, D_MODEL), ATTN_OUT),
        "w_o": nrm(ks[12], (DEPTH, D_MODEL, D_MODEL), D_MODEL),
        "rel_bias": 0.5 * jax.random.normal(ks[13], (NUM_BUCKETS, N_ATTN_HEADS), f32),
        "final_g": 1.0 + 0.05 * jax.random.normal(ks[14], (D_MODEL,), f32),
    }


def reference(x, c, ada_w, ada_b, norm_g, ffn_w_gate, ffn_w_up, ffn_w_down, w_in, conv_w,
              w_conv_out, w_attn_out, w_o, rel_bias, final_g):
    cs = jax.nn.silu(c)
    B = c.shape[0]
    for l in range(DEPTH):
        mod = (cs @ ada_w[l] + ada_b[l]).reshape(B, N_SUB, 3, D_MODEL)[:, :, :, None, :]
        h = _rmsnorm(x, norm_g[l, 0]) * (1.0 + mod[:, 0, 1]) + mod[:, 0, 0]
        x = x + 0.5 * mod[:, 0, 2] * _swiglu(h, ffn_w_gate[l, 0], ffn_w_up[l, 0], ffn_w_down[l, 0])
        h = _rmsnorm(x, norm_g[l, 1]) * (1.0 + mod[:, 1, 1]) + mod[:, 1, 0]
        x = x + mod[:, 1, 2] * _mixer(h, w_in[l], conv_w[l], w_conv_out[l], w_attn_out[l], w_o[l], rel_bias)
        h = _rmsnorm(x, norm_g[l, 2]) * (1.0 + mod[:, 2, 1]) + mod[:, 2, 0]
        x = x + 0.5 * mod[:, 2, 2] * _swiglu(h, ffn_w_gate[l, 1], ffn_w_up[l, 1], ffn_w_down[l, 1])
    return _rmsnorm(x, final_g)
```

```python
import functools

import numpy as np
import jax
import jax.numpy as jnp
from jax import lax
from jax.experimental import pallas as pl
from jax.experimental.pallas import tpu as pltpu

F32 = jnp.float32
BF16 = jnp.bfloat16

HEAD_DIM = 64
HEADS_PER_GROUP = 8
DILATION_GROUPS = ((128, 1), (512, 4), (2048, 16))
N_GROUPS = len(DILATION_GROUPS)
GROUP_WIDTH = HEADS_PER_GROUP * HEAD_DIM
QKV_WIDTH = N_GROUPS * GROUP_WIDTH
NUM_BUCKETS = 32
MAX_DISTANCE = 2048
BLOCK = 128
CONV_K = 3
N_SUB = 3
EPS = 1e-6
NEG_INF = -1e30

V7X_VMEM_BYTES = 64 * 1024 * 1024
LANES = 128
F32_SUBLANES = 8

TOKEN_TILE = 512
ADA_COL_TILE = 1152


def _vmem_limit(nbytes):
    return int(min(nbytes * 5 // 4 + (4 << 20), V7X_VMEM_BYTES - (6 << 20)))


def _sigmoid(x):
    return 1.0 / (1.0 + jnp.exp(-x))


def _norm_mod(x, g, shift, scale):
    ms = jnp.mean(x * x, axis=-1, keepdims=True)
    y = x * lax.rsqrt(ms + EPS) * g
    return y * (1.0 + scale) + shift


def _mod_rows(mod_ref, sub):
    base = 3 * sub
    return (mod_ref[0, 0, base:base + 1, :], mod_ref[0, 0, base + 1:base + 2, :],
            mod_ref[0, 0, base + 2:base + 3, :])


def _resident(shape):
    return pl.BlockSpec(shape, lambda *_: (0,) * len(shape), pipeline_mode=pl.Buffered(1))


def _ada_kernel(c_ref, w_ref, b_ref, o_ref):
    c = c_ref[...]
    cs = (c * _sigmoid(c)).astype(BF16)
    o_ref[0] = jnp.dot(cs, w_ref[0].astype(BF16), preferred_element_type=F32) + b_ref[0]


def _ada_mod(c, ada_w, ada_b):
    depth, d, n = ada_w.shape
    b = c.shape[0]
    tn = ADA_COL_TILE
    out = pl.pallas_call(
        _ada_kernel,
        grid=(depth, n // tn),
        in_specs=[pl.BlockSpec((b, d), lambda l, j: (0, 0)),
                  pl.BlockSpec((1, d, tn), lambda l, j: (l, 0, j)),
                  pl.BlockSpec((1, 1, tn), lambda l, j: (l, 0, j))],
        out_specs=pl.BlockSpec((1, b, tn), lambda l, j: (l, 0, j)),
        out_shape=jax.ShapeDtypeStruct((depth, b, n), F32),
        compiler_params=pltpu.CompilerParams(
            dimension_semantics=("arbitrary", "arbitrary"),
            vmem_limit_bytes=_vmem_limit(2 * d * tn * 4 + d * tn * 2)),
        name="ada_mod",
    )(c, ada_w, ada_b.reshape(depth, 1, n))
    return out.reshape(depth, b, N_SUB * 3, d)


def _t5_bucket(dist):
    exact = NUM_BUCKETS // 2
    d = np.maximum(dist, 1).astype(np.float32)
    large = exact + (np.log(d / exact) / np.log(MAX_DISTANCE / exact) * (NUM_BUCKETS - exact)).astype(np.int32)
    large = np.minimum(large, NUM_BUCKETS - 1)
    return np.where(dist < exact, dist, large).astype(np.int32)


def _bucket_maps():
    i = np.arange(BLOCK)[:, None]
    j = np.arange(2 * BLOCK)[None, :]
    rel = np.maximum(i - j + BLOCK, 0)
    return np.stack([_t5_bucket(rel * dil) for _, dil in DILATION_GROUPS]).astype(np.int32)


def _bias_kernel(tab_ref, bucket_ref, o_ref):
    g = pl.program_id(0)
    bk = bucket_ref[0]
    for h in range(HEADS_PER_GROUP):
        acc = jnp.zeros(bk.shape, F32)
        for b in range(NUM_BUCKETS):
            acc = jnp.where(bk == b, tab_ref[b, g * HEADS_PER_GROUP + h], acc)
        o_ref[0, h] = acc


def _t5_bias(rel_bias):
    buckets = jnp.asarray(_bucket_maps())
    return pl.pallas_call(
        _bias_kernel,
        grid=(N_GROUPS,),
        in_specs=[pl.BlockSpec(memory_space=pltpu.SMEM),
                  pl.BlockSpec((1, BLOCK, 2 * BLOCK), lambda g: (g, 0, 0))],
        out_specs=pl.BlockSpec((1, HEADS_PER_GROUP, BLOCK, 2 * BLOCK), lambda g: (g, 0, 0, 0)),
        out_shape=jax.ShapeDtypeStruct((N_GROUPS, HEADS_PER_GROUP, BLOCK, 2 * BLOCK), F32),
        compiler_params=pltpu.CompilerParams(dimension_semantics=("arbitrary",)),
        name="t5_bias",
    )(rel_bias, buckets)


def _ffn_kernel(x_ref, mod_ref, g_ref, wg_ref, wu_ref, wd_ref, o_ref, *, sub):
    x = x_ref[...]
    shift, scale, gate = _mod_rows(mod_ref, sub)
    h = _norm_mod(x, g_ref[...], shift, scale).astype(BF16)
    a = jnp.dot(h, wg_ref[...], preferred_element_type=F32)
    u = jnp.dot(h, wu_ref[...], preferred_element_type=F32)
    act = (a * _sigmoid(a) * u).astype(BF16)
    y = jnp.dot(act, wd_ref[...], preferred_element_type=F32)
    o_ref[...] = x + (0.5 * gate) * y


def _ffn(x, mod, layer, sub, g, wg, wu, wd, seq):
    t, d = x.shape
    dff = wg.shape[1]
    tm = TOKEN_TILE
    per_seq = seq // tm
    est = 3 * d * dff * 2 + 4 * tm * d * 4 + tm * dff * 14 + tm * d * 4
    return pl.pallas_call(
        functools.partial(_ffn_kernel, sub=sub),
        grid=(t // tm,),
        in_specs=[pl.BlockSpec((tm, d), lambda i: (i, 0)),
                  pl.BlockSpec((1, 1, N_SUB * 3, d), lambda i: (layer, i // per_seq, 0, 0)),
                  _resident((1, d)), _resident((d, dff)), _resident((d, dff)), _resident((dff, d))],
        out_specs=pl.BlockSpec((tm, d), lambda i: (i, 0)),
        out_shape=jax.ShapeDtypeStruct((t, d), F32),
        compiler_params=pltpu.CompilerParams(dimension_semantics=("arbitrary",),
                                             vmem_limit_bytes=_vmem_limit(est)),
        name="ffn",
    )(x, mod, g, wg, wu, wd)


def _inproj_kernel(x_ref, mod_ref, g_ref, w_ref, cw_ref, qkv_ref, yc_ref, gate_ref, pbuf, *, per_seq, d):
    tm = x_ref.shape[0]
    shift, scale, _ = _mod_rows(mod_ref, 1)
    h = _norm_mod(x_ref[...], g_ref[...], shift, scale).astype(BF16)

    def proj(lo, width):
        return jnp.dot(h, w_ref[:, lo:lo + width], preferred_element_type=F32)

    qkv_ref[:, 0:QKV_WIDTH] = (proj(0, QKV_WIDTH) * (HEAD_DIM ** -0.5)).astype(BF16)
    qkv_ref[:, QKV_WIDTH:2 * QKV_WIDTH] = proj(QKV_WIDTH, QKV_WIDTH).astype(BF16)
    qkv_ref[:, 2 * QKV_WIDTH:3 * QKV_WIDTH] = proj(2 * QKV_WIDTH, QKV_WIDTH).astype(BF16)

    base = 3 * QKV_WIDTH
    cb = proj(base, d)
    p = proj(base + d, d) * proj(base + 2 * d, d)

    @pl.when(pl.program_id(0) % per_seq == 0)
    def _():
        pbuf[0:F32_SUBLANES, :] = jnp.zeros((F32_SUBLANES, d), F32)

    pbuf[F32_SUBLANES:F32_SUBLANES + tm, :] = p
    conv = (cw_ref[0:1, :] * pbuf[F32_SUBLANES - 2:F32_SUBLANES - 2 + tm, :]
            + cw_ref[1:2, :] * pbuf[F32_SUBLANES - 1:F32_SUBLANES - 1 + tm, :]
            + cw_ref[2:3, :] * p)
    yc_ref[...] = (cb * conv).astype(BF16)
    pbuf[0:F32_SUBLANES, :] = pbuf[tm:tm + F32_SUBLANES, :]

    gate_ref[...] = proj(base + 3 * d, 2 * d).astype(BF16)


def _inproj(x, mod, layer, g, w_in, conv_w, seq):
    t, d = x.shape
    n = w_in.shape[1]
    tm = TOKEN_TILE
    per_seq = seq // tm
    est = d * n * 2 + 2 * tm * d * 4 + 2 * tm * (3 * QKV_WIDTH + 3 * d) * 2 + (tm + 8) * d * 4 + tm * d * 4 * 8
    return pl.pallas_call(
        functools.partial(_inproj_kernel, per_seq=per_seq, d=d),
        grid=(t // tm,),
        in_specs=[pl.BlockSpec((tm, d), lambda i: (i, 0)),
                  pl.BlockSpec((1, 1, N_SUB * 3, d), lambda i: (layer, i // per_seq, 0, 0)),
                  _resident((1, d)), _resident((d, n)), _resident((CONV_K, d))],
        out_specs=[pl.BlockSpec((tm, 3 * QKV_WIDTH), lambda i: (i, 0)),
                   pl.BlockSpec((tm, d), lambda i: (i, 0)),
                   pl.BlockSpec((tm, 2 * d), lambda i: (i, 0))],
        out_shape=[jax.ShapeDtypeStruct((t, 3 * QKV_WIDTH), BF16),
                   jax.ShapeDtypeStruct((t, d), BF16),
                   jax.ShapeDtypeStruct((t, 2 * d), BF16)],
        scratch_shapes=[pltpu.VMEM((tm + F32_SUBLANES, d), F32)],
        compiler_params=pltpu.CompilerParams(dimension_semantics=("arbitrary",),
                                             vmem_limit_bytes=_vmem_limit(est)),
        name="inproj",
    )(x, mod, g, w_in, conv_w)


def _attn_kernel(q_ref, kp_ref, kc_ref, vp_ref, vc_ref, bias_ref, o_ref, lse_ref):
    n = pl.program_id(1)
    ii = lax.broadcasted_iota(jnp.int32, (BLOCK, 2 * BLOCK), 0)
    jj = lax.broadcasted_iota(jnp.int32, (BLOCK, 2 * BLOCK), 1)
    first_key = jnp.where(n == 0, BLOCK, 0)
    mask = (jj >= jnp.maximum(ii, first_key)) & (jj <= ii + BLOCK)
    lane = lax.broadcasted_iota(jnp.int32, (1, LANES), 1)
    low = lane < HEAD_DIM
    keep_low = low.astype(BF16)
    keep_high = 1 - keep_low
    nt = (((1,), (1,)), ((), ()))

    for hp in range(GROUP_WIDTH // LANES):
        sl = slice(hp * LANES, (hp + 1) * LANES)
        q2 = q_ref[:, sl]
        k2 = jnp.concatenate([kp_ref[:, sl], kc_ref[:, sl]], axis=0)
        v2 = jnp.concatenate([vp_ref[:, sl], vc_ref[:, sl]], axis=0)
        outs, lses, sums = [], [], []
        for half, keep in enumerate((keep_low, keep_high)):
            s = lax.dot_general(q2 * keep, k2, nt, preferred_element_type=F32)
            s = jnp.where(mask, s + bias_ref[0, 2 * hp + half], NEG_INF)
            m = jnp.max(s, axis=-1, keepdims=True)
            p = jnp.exp(s - m)
            l = jnp.sum(p, axis=-1, keepdims=True)
            outs.append(jnp.dot(p.astype(BF16), v2, preferred_element_type=F32))
            sums.append(l)
            lses.append(m + jnp.log(l))
        o = jnp.where(low, outs[0], outs[1]) / jnp.where(low, sums[0], sums[1])
        o_ref[:, sl] = o.astype(BF16)
        lse_ref[:, sl] = jnp.where(low, lses[0], lses[1])


def _attention_group(qkv, bias, group, batch, seq):
    t = qkv.shape[0]
    dil = DILATION_GROUPS[group][1]
    sub_len = seq // dil
    nb = sub_len // BLOCK
    cols = 3 * QKV_WIDTH // GROUP_WIDTH
    qkv_d = qkv.reshape(t // dil, dil * 3 * QKV_WIDTH)

    def cur(off):
        return pl.BlockSpec((BLOCK, GROUP_WIDTH), lambda b, n, r: (b * nb + n, r * cols + off + group))

    def prev(off):
        return pl.BlockSpec((BLOCK, GROUP_WIDTH),
                            lambda b, n, r: (b * nb + jnp.maximum(n - 1, 0), r * cols + off + group))

    out_spec = pl.BlockSpec((BLOCK, GROUP_WIDTH), lambda b, n, r: (b * nb + n, r))
    o, lse = pl.pallas_call(
        _attn_kernel,
        grid=(batch, nb, dil),
        in_specs=[cur(0), prev(N_GROUPS), cur(N_GROUPS), prev(2 * N_GROUPS), cur(2 * N_GROUPS),
                  pl.BlockSpec((1, HEADS_PER_GROUP, BLOCK, 2 * BLOCK), lambda b, n, r: (group, 0, 0, 0))],
        out_specs=[out_spec, out_spec],
        out_shape=[jax.ShapeDtypeStruct((t // dil, dil * GROUP_WIDTH), BF16),
                   jax.ShapeDtypeStruct((t // dil, dil * GROUP_WIDTH), F32)],
        compiler_params=pltpu.CompilerParams(dimension_semantics=("arbitrary", "arbitrary", "arbitrary")),
        name=f"attn_g{group}",
    )(qkv_d, qkv_d, qkv_d, qkv_d, qkv_d, bias)
    return o.reshape(t, GROUP_WIDTH), lse.reshape(t, GROUP_WIDTH)


def _mixout_kernel(x_ref, mod_ref, o1_ref, o2_ref, o3_ref, l1_ref, l2_ref, l3_ref, yc_ref, gate_ref,
                   wao_ref, wco_ref, wo_ref, out_ref, *, d):
    _, _, gate = _mod_rows(mod_ref, 1)
    l1, l2, l3 = l1_ref[...], l2_ref[...], l3_ref[...]
    m = jnp.maximum(jnp.maximum(l1, l2), l3)
    e1, e2, e3 = jnp.exp(l1 - m), jnp.exp(l2 - m), jnp.exp(l3 - m)
    mix = (e1 * o1_ref[...].astype(F32) + e2 * o2_ref[...].astype(F32) + e3 * o3_ref[...].astype(F32))
    mix = mix / (e1 + e2 + e3)
    y_attn = jnp.dot(mix.astype(BF16), wao_ref[...], preferred_element_type=F32)
    y_conv = jnp.dot(yc_ref[...], wco_ref[...], preferred_element_type=F32)
    g_conv = gate_ref[:, 0:d].astype(F32)
    g_attn = gate_ref[:, d:2 * d].astype(F32)
    merged = _sigmoid(g_conv) * y_conv + _sigmoid(g_attn) * y_attn
    y = jnp.dot(merged.astype(BF16), wo_ref[...], preferred_element_type=F32)
    out_ref[...] = x_ref[...] + gate * y


def _mixout(x, mod, layer, outs, lses, yc, gates, wao, wco, wo, seq):
    t, d = x.shape
    tm = TOKEN_TILE
    per_seq = seq // tm
    row = lambda w: pl.BlockSpec((tm, w), lambda i: (i, 0))
    est = (2 * d * d + GROUP_WIDTH * d) * 2 + 2 * tm * (2 * d * 4 + 3 * GROUP_WIDTH * 6 + 3 * d * 2) + tm * d * 4 * 6
    return pl.pallas_call(
        functools.partial(_mixout_kernel, d=d),
        grid=(t // tm,),
        in_specs=[row(d), pl.BlockSpec((1, 1, N_SUB * 3, d), lambda i: (layer, i // per_seq, 0, 0)),
                  row(GROUP_WIDTH), row(GROUP_WIDTH), row(GROUP_WIDTH),
                  row(GROUP_WIDTH), row(GROUP_WIDTH), row(GROUP_WIDTH),
                  row(d), row(2 * d),
                  _resident((GROUP_WIDTH, d)), _resident((d, d)), _resident((d, d))],
        out_specs=row(d),
        out_shape=jax.ShapeDtypeStruct((t, d), F32),
        compiler_params=pltpu.CompilerParams(dimension_semantics=("arbitrary",),
                                             vmem_limit_bytes=_vmem_limit(est)),
        name="mixout",
    )(x, mod, *outs, *lses, yc, gates, wao, wco, wo)


def _final_kernel(x_ref, g_ref, o_ref):
    x = x_ref[...]
    ms = jnp.mean(x * x, axis=-1, keepdims=True)
    o_ref[...] = x * lax.rsqrt(ms + EPS) * g_ref[...]


def _final_norm(x, g):
    t, d = x.shape
    tm = TOKEN_TILE
    return pl.pallas_call(
        _final_kernel,
        grid=(t // tm,),
        in_specs=[pl.BlockSpec((tm, d), lambda i: (i, 0)), _resident((1, d))],
        out_specs=pl.BlockSpec((tm, d), lambda i: (i, 0)),
        out_shape=jax.ShapeDtypeStruct((t, d), F32),
        compiler_params=pltpu.CompilerParams(dimension_semantics=("arbitrary",)),
        name="final_norm",
    )(x, g)


def kernel(x, c, ada_w, ada_b, norm_g, ffn_w_gate, ffn_w_up, ffn_w_down, w_in, conv_w, w_conv_out, w_attn_out,
           w_o, rel_bias, final_g):
    batch, seq, d = x.shape
    depth = ada_w.shape[0]
    assert seq % TOKEN_TILE == 0 and all(seq % (dil * BLOCK) == 0 for _, dil in DILATION_GROUPS)
    assert all(win // dil == BLOCK for win, dil in DILATION_GROUPS)
    assert w_in.shape[-1] == 3 * QKV_WIDTH + 5 * d and ada_w.shape[-1] == N_SUB * 3 * d

    mod = _ada_mod(c, ada_w, ada_b)
    bias = _t5_bias(rel_bias)
    xt = x.reshape(batch * seq, d)
    for l in range(depth):
        g = [norm_g[l, j].reshape(1, d) for j in range(N_SUB)]
        xt = _ffn(xt, mod, l, 0, g[0], ffn_w_gate[l, 0].astype(BF16), ffn_w_up[l, 0].astype(BF16),
                  ffn_w_down[l, 0].astype(BF16), seq)
        qkv, yc, gates = _inproj(xt, mod, l, g[1], w_in[l].astype(BF16), conv_w[l], seq)
        outs, lses = zip(*[_attention_group(qkv, bias, grp, batch, seq) for grp in range(N_GROUPS)])
        xt = _mixout(xt, mod, l, outs, lses, yc, gates, w_attn_out[l].astype(BF16),
                     w_conv_out[l].astype(BF16), w_o[l].astype(BF16), seq)
        xt = _ffn(xt, mod, l, 2, g[2], ffn_w_gate[l, 1].astype(BF16), ffn_w_up[l, 1].astype(BF16),
                  ffn_w_down[l, 1].astype(BF16), seq)
    return _final_norm(xt, final_g.reshape(1, d)).reshape(batch, seq, d)
```

```python
import functools

import numpy as np
import jax
import jax.numpy as jnp
from jax import lax
from jax.experimental import pallas as pl
from jax.experimental.pallas import tpu as pltpu

F32 = jnp.float32
BF16 = jnp.bfloat16

HEAD_DIM = 64
HEADS_PER_GROUP = 8
DILATION_GROUPS = ((128, 1), (512, 4), (2048, 16))
N_GROUPS = len(DILATION_GROUPS)
GROUP_WIDTH = HEADS_PER_GROUP * HEAD_DIM
QKV_WIDTH = N_GROUPS * GROUP_WIDTH
NUM_BUCKETS = 32
MAX_DISTANCE = 2048
BLOCK = 128
CONV_K = 3
N_SUB = 3
EPS = 1e-6
NEG_INF = -1e30

V7X_VMEM_BYTES = 64 * 1024 * 1024
LANES = 128
F32_SUBLANES = 8

TOKEN_TILE = 512
ADA_COL_TILE = 1152
MAX_DILATION = max(d for _, d in DILATION_GROUPS)
ATTN_TILE = BLOCK * MAX_DILATION
SLABS = GROUP_WIDTH // LANES


def _vmem_limit(nbytes):
    return int(min(nbytes * 5 // 4 + (4 << 20), V7X_VMEM_BYTES - (6 << 20)))


def _sigmoid(x):
    return 1.0 / (1.0 + jnp.exp(-x))


def _norm_mod(x, g, shift, scale):
    ms = jnp.mean(x * x, axis=-1, keepdims=True)
    y = x * lax.rsqrt(ms + EPS) * g
    return y * (1.0 + scale) + shift


def _mod_rows(mod_ref, sub):
    base = 3 * sub
    return (mod_ref[0, 0, base:base + 1, :], mod_ref[0, 0, base + 1:base + 2, :],
            mod_ref[0, 0, base + 2:base + 3, :])


def _resident(shape):
    return pl.BlockSpec(shape, lambda *_: (0,) * len(shape), pipeline_mode=pl.Buffered(1))


def _ada_kernel(c_ref, w_ref, b_ref, o_ref):
    c = c_ref[...]
    cs = (c * _sigmoid(c)).astype(BF16)
    o_ref[0] = jnp.dot(cs, w_ref[0].astype(BF16), preferred_element_type=F32) + b_ref[0]


def _ada_mod(c, ada_w, ada_b):
    depth, d, n = ada_w.shape
    b = c.shape[0]
    tn = ADA_COL_TILE
    out = pl.pallas_call(
        _ada_kernel,
        grid=(depth, n // tn),
        in_specs=[pl.BlockSpec((b, d), lambda l, j: (0, 0)),
                  pl.BlockSpec((1, d, tn), lambda l, j: (l, 0, j)),
                  pl.BlockSpec((1, 1, tn), lambda l, j: (l, 0, j))],
        out_specs=pl.BlockSpec((1, b, tn), lambda l, j: (l, 0, j)),
        out_shape=jax.ShapeDtypeStruct((depth, b, n), F32),
        compiler_params=pltpu.CompilerParams(
            dimension_semantics=("arbitrary", "arbitrary"),
            vmem_limit_bytes=_vmem_limit(2 * d * tn * 4 + d * tn * 2)),
        name="ada_mod",
    )(c, ada_w, ada_b.reshape(depth, 1, n))
    return out.reshape(depth, b, N_SUB * 3, d)


def _t5_bucket(dist):
    exact = NUM_BUCKETS // 2
    d = np.maximum(dist, 1).astype(np.float32)
    large = exact + (np.log(d / exact) / np.log(MAX_DISTANCE / exact) * (NUM_BUCKETS - exact)).astype(np.int32)
    large = np.minimum(large, NUM_BUCKETS - 1)
    return np.where(dist < exact, dist, large).astype(np.int32)


def _bucket_maps():
    i = np.arange(BLOCK)[:, None]
    j = np.arange(2 * BLOCK)[None, :]
    rel = np.maximum(i - j + BLOCK, 0)
    return np.stack([_t5_bucket(rel * dil) for _, dil in DILATION_GROUPS]).astype(np.int32)


def _bias_kernel(tab_ref, bucket_ref, o_ref):
    g = pl.program_id(0)
    bk = bucket_ref[0]
    for h in range(HEADS_PER_GROUP):
        acc = jnp.zeros(bk.shape, F32)
        for b in range(NUM_BUCKETS):
            acc = jnp.where(bk == b, tab_ref[b, g * HEADS_PER_GROUP + h], acc)
        o_ref[0, h] = acc


def _t5_bias(rel_bias):
    buckets = jnp.asarray(_bucket_maps())
    return pl.pallas_call(
        _bias_kernel,
        grid=(N_GROUPS,),
        in_specs=[pl.BlockSpec(memory_space=pltpu.SMEM),
                  pl.BlockSpec((1, BLOCK, 2 * BLOCK), lambda g: (g, 0, 0))],
        out_specs=pl.BlockSpec((1, HEADS_PER_GROUP, BLOCK, 2 * BLOCK), lambda g: (g, 0, 0, 0)),
        out_shape=jax.ShapeDtypeStruct((N_GROUPS, HEADS_PER_GROUP, BLOCK, 2 * BLOCK), F32),
        compiler_params=pltpu.CompilerParams(dimension_semantics=("arbitrary",)),
        name="t5_bias",
    )(rel_bias, buckets)


def _ffn_kernel(x_ref, mod_ref, g_ref, wg_ref, wu_ref, wd_ref, o_ref, *, sub):
    x = x_ref[...]
    shift, scale, gate = _mod_rows(mod_ref, sub)
    h = _norm_mod(x, g_ref[...], shift, scale).astype(BF16)
    a = jnp.dot(h, wg_ref[...], preferred_element_type=F32)
    u = jnp.dot(h, wu_ref[...], preferred_element_type=F32)
    act = (a * _sigmoid(a) * u).astype(BF16)
    y = jnp.dot(act, wd_ref[...], preferred_element_type=F32)
    o_ref[...] = x + (0.5 * gate) * y


def _ffn(x, mod, layer, sub, g, wg, wu, wd, seq):
    t, d = x.shape
    dff = wg.shape[1]
    tm = TOKEN_TILE
    per_seq = seq // tm
    est = 3 * d * dff * 2 + 4 * tm * d * 4 + tm * dff * 14 + tm * d * 4
    return pl.pallas_call(
        functools.partial(_ffn_kernel, sub=sub),
        grid=(t // tm,),
        in_specs=[pl.BlockSpec((tm, d), lambda i: (i, 0)),
                  pl.BlockSpec((1, 1, N_SUB * 3, d), lambda i: (layer, i // per_seq, 0, 0)),
                  _resident((1, d)), _resident((d, dff)), _resident((d, dff)), _resident((dff, d))],
        out_specs=pl.BlockSpec((tm, d), lambda i: (i, 0)),
        out_shape=jax.ShapeDtypeStruct((t, d), F32),
        compiler_params=pltpu.CompilerParams(dimension_semantics=("arbitrary",),
                                             vmem_limit_bytes=_vmem_limit(est)),
        name="ffn",
    )(x, mod, g, wg, wu, wd)


def _inproj_kernel(x_ref, mod_ref, g_ref, w_ref, cw_ref, qkv1_ref, qkv2_ref, qkv3_ref, yc_ref, gate_ref,
                   pbuf, ubuf, *, per_seq, d):
    tm = x_ref.shape[0]
    shift, scale, _ = _mod_rows(mod_ref, 1)
    h = _norm_mod(x_ref[...], g_ref[...], shift, scale).astype(BF16)

    def proj(lo, width):
        return jnp.dot(h, w_ref[:, lo:lo + width], preferred_element_type=F32)

    outs = (qkv1_ref, qkv2_ref, qkv3_ref)
    for part in range(3):
        u = proj(part * QKV_WIDTH, QKV_WIDTH)
        if part == 0:
            u = u * (HEAD_DIM ** -0.5)
        col = part * GROUP_WIDTH
        for grp, (_, dil) in enumerate(DILATION_GROUPS):
            ug = u[:, grp * GROUP_WIDTH:(grp + 1) * GROUP_WIDTH]
            if dil == 1:
                outs[grp][:, col:col + GROUP_WIDTH] = ug.astype(BF16)
                continue
            run = tm // dil
            for s in range(SLABS):
                ubuf[s] = ug[:, s * LANES:(s + 1) * LANES]
            for s in range(SLABS):
                for r in range(dil):
                    outs[grp][r * run:(r + 1) * run, col + s * LANES:col + (s + 1) * LANES] = (
                        ubuf[s, pl.ds(r, run, stride=dil), :].astype(BF16))

    base = 3 * QKV_WIDTH
    cb = proj(base, d)
    p = proj(base + d, d) * proj(base + 2 * d, d)

    @pl.when(pl.program_id(0) % per_seq == 0)
    def _():
        pbuf[0:F32_SUBLANES, :] = jnp.zeros((F32_SUBLANES, d), F32)

    pbuf[F32_SUBLANES:F32_SUBLANES + tm, :] = p
    conv = (cw_ref[0:1, :] * pbuf[F32_SUBLANES - 2:F32_SUBLANES - 2 + tm, :]
            + cw_ref[1:2, :] * pbuf[F32_SUBLANES - 1:F32_SUBLANES - 1 + tm, :]
            + cw_ref[2:3, :] * p)
    yc_ref[...] = (cb * conv).astype(BF16)
    pbuf[0:F32_SUBLANES, :] = pbuf[tm:tm + F32_SUBLANES, :]

    gate_ref[...] = proj(base + 3 * d, 2 * d).astype(BF16)


def _inproj(x, mod, layer, g, w_in, conv_w, seq):
    t, d = x.shape
    n = w_in.shape[1]
    tm = TOKEN_TILE
    per_seq = seq // tm
    est = (d * n * 2 + 2 * tm * d * 4 + 2 * tm * (3 * QKV_WIDTH + 3 * d) * 2 + (tm + 8) * d * 4
           + SLABS * tm * LANES * 4 + tm * d * 4 * 8)
    row = lambda w: pl.BlockSpec((tm, w), lambda i: (i, 0))
    return pl.pallas_call(
        functools.partial(_inproj_kernel, per_seq=per_seq, d=d),
        grid=(t // tm,),
        in_specs=[row(d),
                  pl.BlockSpec((1, 1, N_SUB * 3, d), lambda i: (layer, i // per_seq, 0, 0)),
                  _resident((1, d)), _resident((d, n)), _resident((CONV_K, d))],
        out_specs=[row(QKV_WIDTH), row(QKV_WIDTH), row(QKV_WIDTH), row(d), row(2 * d)],
        out_shape=[jax.ShapeDtypeStruct((t, QKV_WIDTH), BF16)] * 3
                  + [jax.ShapeDtypeStruct((t, d), BF16), jax.ShapeDtypeStruct((t, 2 * d), BF16)],
        scratch_shapes=[pltpu.VMEM((tm + F32_SUBLANES, d), F32), pltpu.VMEM((SLABS, tm, LANES), F32)],
        compiler_params=pltpu.CompilerParams(dimension_semantics=("arbitrary",),
                                             vmem_limit_bytes=_vmem_limit(est)),
        name="inproj",
    )(x, mod, g, w_in, conv_w)


def _attn_kernel(q1, k1, v1, q2, k2, v2, q3, k3, v3, bias_ref, out_ref,
                 pk1, pv1, pk2, pv2, pk3, pv3, o1, l1, o2, l2, o3, l3):
    n = pl.program_id(2)
    ii = lax.broadcasted_iota(jnp.int32, (BLOCK, 2 * BLOCK), 0)
    jj = lax.broadcasted_iota(jnp.int32, (BLOCK, 2 * BLOCK), 1)
    band_prev = (jj >= ii) & (jj <= ii + BLOCK)
    band_cur = (lax.broadcasted_iota(jnp.int32, (BLOCK, BLOCK), 1)
                <= lax.broadcasted_iota(jnp.int32, (BLOCK, BLOCK), 0))
    lane = lax.broadcasted_iota(jnp.int32, (1, LANES), 1)
    low = lane < HEAD_DIM
    keeps = (low.astype(BF16), 1 - low.astype(BF16))
    nt = (((1,), (1,)), ((), ()))

    def unit(grp, q, k, v):
        nk = k.shape[0]
        with_prev = nk == 2 * BLOCK
        band = band_prev if with_prev else band_cur
        v_ext = jnp.concatenate([v, jnp.ones((nk, LANES), BF16)], axis=1)
        parts = []
        for half in range(2):
            s = lax.dot_general(q * keeps[half], k, nt, preferred_element_type=F32)
            b = bias_ref[grp, half] if with_prev else bias_ref[grp, half, :, BLOCK:]
            s = jnp.where(band, s + b, NEG_INF)
            m = jnp.max(s, axis=-1, keepdims=True)
            p = jnp.exp(s - m).astype(BF16)
            oe = jnp.dot(p, v_ext, preferred_element_type=F32)
            parts.append((oe[:, :LANES], oe[:, LANES:], m))
        (oa, la, ma), (ob, lb, mb) = parts
        l = jnp.where(low, la, lb)
        return jnp.where(low, oa, ob) / l, jnp.where(low, ma, mb) + jnp.log(l)

    def rows(ref, start, size=BLOCK):
        return ref[pl.ds(pl.multiple_of(start, 32), size), :]

    def cat(a, b):
        return jnp.concatenate([a, b], axis=0)

    def g1_first(with_prev):
        k, v = k1[0:BLOCK, :], v1[0:BLOCK, :]
        if with_prev:
            k, v = cat(pk1[...], k), cat(pv1[...], v)
        o1[0:BLOCK, :], l1[0:BLOCK, :] = unit(0, q1[0:BLOCK, :], k, v)

    def g1_rest(u, carry):
        start = u * BLOCK
        o, lse = unit(0, rows(q1, start), rows(k1, start - BLOCK, 2 * BLOCK), rows(v1, start - BLOCK, 2 * BLOCK))
        o1[pl.ds(pl.multiple_of(start, BLOCK), BLOCK), :] = o
        l1[pl.ds(pl.multiple_of(start, BLOCK), BLOCK), :] = lse
        return carry

    d2 = DILATION_GROUPS[1][1]

    def g2_store(n2, r, o, lse):
        o2[pl.ds(n2 * TOKEN_TILE + r, BLOCK, stride=d2), :] = o
        l2[pl.ds(n2 * TOKEN_TILE + r, BLOCK, stride=d2), :] = lse

    def g2_first(with_prev):
        for r in range(d2):
            sl = slice(r * BLOCK, (r + 1) * BLOCK)
            k, v = k2[sl, :], v2[sl, :]
            if with_prev:
                k, v = cat(pk2[sl, :], k), cat(pv2[sl, :], v)
            g2_store(0, r, *unit(1, q2[sl, :], k, v))

    def g2_rest(u, carry):
        n2, r = lax.div(u, d2), lax.rem(u, d2)
        cur = n2 * TOKEN_TILE + r * BLOCK
        prev = cur - TOKEN_TILE
        o, lse = unit(1, rows(q2, cur), cat(rows(k2, prev), rows(k2, cur)), cat(rows(v2, prev), rows(v2, cur)))
        g2_store(n2, r, o, lse)
        return carry

    d3 = DILATION_GROUPS[2][1]
    run3 = TOKEN_TILE // d3

    def gather3(ref, r):
        return jnp.concatenate([rows(ref, j * TOKEN_TILE + r * run3, run3) for j in range(ATTN_TILE // TOKEN_TILE)],
                               axis=0)

    def g3_unit(with_prev):
        def body(r, carry):
            k, v = gather3(k3, r), gather3(v3, r)
            if with_prev:
                k, v = cat(gather3(pk3, r), k), cat(gather3(pv3, r), v)
            o, lse = unit(2, gather3(q3, r), k, v)
            o3[pl.ds(r, BLOCK, stride=d3), :] = o
            l3[pl.ds(r, BLOCK, stride=d3), :] = lse
            return carry
        return body

    @pl.when(n == 0)
    def _():
        g1_first(False)
        g2_first(False)
        lax.fori_loop(0, d3, g3_unit(False), 0)
        pk1[...] = k1[ATTN_TILE - BLOCK:, :]
        pv1[...] = v1[ATTN_TILE - BLOCK:, :]
        pk2[...] = k2[ATTN_TILE - TOKEN_TILE:, :]
        pv2[...] = v2[ATTN_TILE - TOKEN_TILE:, :]
        pk3[...] = k3[...]
        pv3[...] = v3[...]

    @pl.when(n != 0)
    def _():
        g1_first(True)
        g2_first(True)
        lax.fori_loop(0, d3, g3_unit(True), 0)

    lax.fori_loop(1, ATTN_TILE // BLOCK, g1_rest, 0)
    lax.fori_loop(d2, ATTN_TILE // BLOCK, g2_rest, 0)

    la, lb, lc = l1[...], l2[...], l3[...]
    m = jnp.maximum(jnp.maximum(la, lb), lc)
    ea, eb, ec = jnp.exp(la - m), jnp.exp(lb - m), jnp.exp(lc - m)
    mix = (ea * o1[...] + eb * o2[...] + ec * o3[...]) / (ea + eb + ec)
    out_ref[...] = mix.astype(BF16)


def _attention(qkvs, bias, batch, seq):
    t = qkvs[0].shape[0]
    assert seq == 2 * ATTN_TILE, "the previous-tile carry assumes two attention tiles per sequence"
    per_seq = seq // ATTN_TILE

    def part(p):
        return pl.BlockSpec((ATTN_TILE, LANES), lambda b, s, n: (b * per_seq + n, p * SLABS + s))

    in_specs = [part(p) for _ in range(N_GROUPS) for p in range(3)]
    in_specs.append(pl.BlockSpec((N_GROUPS, 2, BLOCK, 2 * BLOCK), lambda b, s, n: (0, s, 0, 0)))
    tails = [BLOCK, BLOCK, TOKEN_TILE, TOKEN_TILE, ATTN_TILE, ATTN_TILE]
    scratch = [pltpu.VMEM((rows, LANES), BF16) for rows in tails]
    scratch += [pltpu.VMEM((ATTN_TILE, LANES), F32)] * (2 * N_GROUPS)
    est = (2 * 9 * ATTN_TILE * LANES * 2 + 2 * N_GROUPS * 2 * BLOCK * 2 * BLOCK * 4 + sum(tails) * LANES * 2
           + 2 * N_GROUPS * ATTN_TILE * LANES * 4 + 8 * ATTN_TILE * LANES * 4)
    operands = [a for a in qkvs for _ in range(3)]
    return pl.pallas_call(
        _attn_kernel,
        grid=(batch, SLABS, per_seq),
        in_specs=in_specs,
        out_specs=pl.BlockSpec((ATTN_TILE, LANES), lambda b, s, n: (b * per_seq + n, s)),
        out_shape=jax.ShapeDtypeStruct((t, GROUP_WIDTH), BF16),
        scratch_shapes=scratch,
        compiler_params=pltpu.CompilerParams(dimension_semantics=("arbitrary", "arbitrary", "arbitrary"),
                                             vmem_limit_bytes=_vmem_limit(est)),
        name="attn",
    )(*operands, bias)


def _mixout_kernel(x_ref, mod_ref, mix_ref, yc_ref, gate_ref, wao_ref, wco_ref, wo_ref, out_ref, *, d):
    _, _, gate = _mod_rows(mod_ref, 1)
    y_attn = jnp.dot(mix_ref[...], wao_ref[...], preferred_element_type=F32)
    y_conv = jnp.dot(yc_ref[...], wco_ref[...], preferred_element_type=F32)
    g_conv = gate_ref[:, 0:d].astype(F32)
    g_attn = gate_ref[:, d:2 * d].astype(F32)
    merged = _sigmoid(g_conv) * y_conv + _sigmoid(g_attn) * y_attn
    y = jnp.dot(merged.astype(BF16), wo_ref[...], preferred_element_type=F32)
    out_ref[...] = x_ref[...] + gate * y


def _mixout(x, mod, layer, mix, yc, gates, wao, wco, wo, seq):
    t, d = x.shape
    tm = TOKEN_TILE
    per_seq = seq // tm
    row = lambda w: pl.BlockSpec((tm, w), lambda i: (i, 0))
    est = (2 * d * d + GROUP_WIDTH * d) * 2 + 2 * tm * (2 * d * 4 + GROUP_WIDTH * 2 + 3 * d * 2) + tm * d * 4 * 6
    return pl.pallas_call(
        functools.partial(_mixout_kernel, d=d),
        grid=(t // tm,),
        in_specs=[row(d), pl.BlockSpec((1, 1, N_SUB * 3, d), lambda i: (layer, i // per_seq, 0, 0)),
                  row(GROUP_WIDTH), row(d), row(2 * d),
                  _resident((GROUP_WIDTH, d)), _resident((d, d)), _resident((d, d))],
        out_specs=row(d),
        out_shape=jax.ShapeDtypeStruct((t, d), F32),
        compiler_params=pltpu.CompilerParams(dimension_semantics=("arbitrary",),
                                             vmem_limit_bytes=_vmem_limit(est)),
        name="mixout",
    )(x, mod, mix, yc, gates, wao, wco, wo)


def _final_kernel(x_ref, g_ref, o_ref):
    x = x_ref[...]
    ms = jnp.mean(x * x, axis=-1, keepdims=True)
    o_ref[...] = x * lax.rsqrt(ms + EPS) * g_ref[...]


def _final_norm(x, g):
    t, d = x.shape
    tm = TOKEN_TILE
    return pl.pallas_call(
        _final_kernel,
        grid=(t // tm,),
        in_specs=[pl.BlockSpec((tm, d), lambda i: (i, 0)), _resident((1, d))],
        out_specs=pl.BlockSpec((tm, d), lambda i: (i, 0)),
        out_shape=jax.ShapeDtypeStruct((t, d), F32),
        compiler_params=pltpu.CompilerParams(dimension_semantics=("arbitrary",)),
        name="final_norm",
    )(x, g)


def kernel(x, c, ada_w, ada_b, norm_g, ffn_w_gate, ffn_w_up, ffn_w_down, w_in, conv_w, w_conv_out, w_attn_out,
           w_o, rel_bias, final_g):
    batch, seq, d = x.shape
    depth = ada_w.shape[0]
    assert seq % ATTN_TILE == 0 and ATTN_TILE % TOKEN_TILE == 0
    assert all(win // dil == BLOCK and TOKEN_TILE % (dil * 32) == 0 or dil == 1 for win, dil in DILATION_GROUPS)
    assert w_in.shape[-1] == 3 * QKV_WIDTH + 5 * d and ada_w.shape[-1] == N_SUB * 3 * d

    mod = _ada_mod(c, ada_w, ada_b)
    bias = _t5_bias(rel_bias)
    xt = x.reshape(batch * seq, d)
    for l in range(depth):
        g = [norm_g[l, j].reshape(1, d) for j in range(N_SUB)]
        xt = _ffn(xt, mod, l, 0, g[0], ffn_w_gate[l, 0].astype(BF16), ffn_w_up[l, 0].astype(BF16),
                  ffn_w_down[l, 0].astype(BF16), seq)
        *qkvs, yc, gates = _inproj(xt, mod, l, g[1], w_in[l].astype(BF16), conv_w[l], seq)
        mix = _attention(qkvs, bias, batch, seq)
        xt = _mixout(xt, mod, l, mix, yc, gates, w_attn_out[l].astype(BF16),
                     w_conv_out[l].astype(BF16), w_o[l].astype(BF16), seq)
        xt = _ffn(xt, mod, l, 2, g[2], ffn_w_gate[l, 1].astype(BF16), ffn_w_up[l, 1].astype(BF16),
                  ffn_w_down[l, 1].astype(BF16), seq)
    return _final_norm(xt, final_g.reshape(1, d)).reshape(batch, seq, d)
```

```python
import functools

import numpy as np
import jax
import jax.numpy as jnp
from jax import lax
from jax.experimental import pallas as pl
from jax.experimental.pallas import tpu as pltpu

F32 = jnp.float32
BF16 = jnp.bfloat16

HEAD_DIM = 64
HEADS_PER_GROUP = 8
DILATION_GROUPS = ((128, 1), (512, 4), (2048, 16))
N_GROUPS = len(DILATION_GROUPS)
GROUP_WIDTH = HEADS_PER_GROUP * HEAD_DIM
QKV_WIDTH = N_GROUPS * GROUP_WIDTH
NUM_BUCKETS = 32
MAX_DISTANCE = 2048
BLOCK = 128
CONV_K = 3
N_SUB = 3
EPS = 1e-6
NEG_INF = -1e30

V7X_VMEM_BYTES = 64 * 1024 * 1024
LANES = 128
F32_SUBLANES = 8

TOKEN_TILE = 512
ADA_COL_TILE = 1152
MAX_DILATION = max(d for _, d in DILATION_GROUPS)
ATTN_TILE = BLOCK * MAX_DILATION
SLABS = GROUP_WIDTH // LANES
ATTN_UNROLL = 4


def _vmem_limit(nbytes):
    return int(min(nbytes * 5 // 4 + (4 << 20), V7X_VMEM_BYTES - (6 << 20)))


def _sigmoid(x):
    return 1.0 / (1.0 + jnp.exp(-x))


def _norm_mod(x, g, shift, scale):
    ms = jnp.mean(x * x, axis=-1, keepdims=True)
    y = x * lax.rsqrt(ms + EPS) * g
    return y * (1.0 + scale) + shift


def _mod_rows(mod_ref, sub):
    base = 3 * sub
    return (mod_ref[0, 0, base:base + 1, :], mod_ref[0, 0, base + 1:base + 2, :],
            mod_ref[0, 0, base + 2:base + 3, :])


def _resident(shape):
    return pl.BlockSpec(shape, lambda *_: (0,) * len(shape), pipeline_mode=pl.Buffered(1))


def _ada_kernel(c_ref, w_ref, b_ref, o_ref):
    c = c_ref[...]
    cs = (c * _sigmoid(c)).astype(BF16)
    o_ref[0] = jnp.dot(cs, w_ref[0].astype(BF16), preferred_element_type=F32) + b_ref[0]


def _ada_mod(c, ada_w, ada_b):
    depth, d, n = ada_w.shape
    b = c.shape[0]
    tn = ADA_COL_TILE
    out = pl.pallas_call(
        _ada_kernel,
        grid=(depth, n // tn),
        in_specs=[pl.BlockSpec((b, d), lambda l, j: (0, 0)),
                  pl.BlockSpec((1, d, tn), lambda l, j: (l, 0, j)),
                  pl.BlockSpec((1, 1, tn), lambda l, j: (l, 0, j))],
        out_specs=pl.BlockSpec((1, b, tn), lambda l, j: (l, 0, j)),
        out_shape=jax.ShapeDtypeStruct((depth, b, n), F32),
        compiler_params=pltpu.CompilerParams(
            dimension_semantics=("arbitrary", "arbitrary"),
            vmem_limit_bytes=_vmem_limit(2 * d * tn * 4 + d * tn * 2)),
        name="ada_mod",
    )(c, ada_w, ada_b.reshape(depth, 1, n))
    return out.reshape(depth, b, N_SUB * 3, d)


def _t5_bucket(dist):
    exact = NUM_BUCKETS // 2
    d = np.maximum(dist, 1).astype(np.float32)
    large = exact + (np.log(d / exact) / np.log(MAX_DISTANCE / exact) * (NUM_BUCKETS - exact)).astype(np.int32)
    large = np.minimum(large, NUM_BUCKETS - 1)
    return np.where(dist < exact, dist, large).astype(np.int32)


def _bucket_maps():
    i = np.arange(BLOCK)[:, None]
    j = np.arange(2 * BLOCK)[None, :]
    rel = np.maximum(i - j + BLOCK, 0)
    return np.stack([_t5_bucket(rel * dil) for _, dil in DILATION_GROUPS]).astype(np.int32)


def _bias_kernel(tab_ref, bucket_ref, o_ref):
    g = pl.program_id(0)
    bk = bucket_ref[0]
    for h in range(HEADS_PER_GROUP):
        acc = jnp.zeros(bk.shape, F32)
        for b in range(NUM_BUCKETS):
            acc = jnp.where(bk == b, tab_ref[b, g * HEADS_PER_GROUP + h], acc)
        o_ref[0, h] = acc


def _t5_bias(rel_bias):
    buckets = jnp.asarray(_bucket_maps())
    return pl.pallas_call(
        _bias_kernel,
        grid=(N_GROUPS,),
        in_specs=[pl.BlockSpec(memory_space=pltpu.SMEM),
                  pl.BlockSpec((1, BLOCK, 2 * BLOCK), lambda g: (g, 0, 0))],
        out_specs=pl.BlockSpec((1, HEADS_PER_GROUP, BLOCK, 2 * BLOCK), lambda g: (g, 0, 0, 0)),
        out_shape=jax.ShapeDtypeStruct((N_GROUPS, HEADS_PER_GROUP, BLOCK, 2 * BLOCK), F32),
        compiler_params=pltpu.CompilerParams(dimension_semantics=("arbitrary",)),
        name="t5_bias",
    )(rel_bias, buckets)


def _ffn_kernel(x_ref, mod_ref, g_ref, wg_ref, wu_ref, wd_ref, o_ref, *, sub):
    x = x_ref[...]
    shift, scale, gate = _mod_rows(mod_ref, sub)
    h = _norm_mod(x, g_ref[...], shift, scale).astype(BF16)
    a = jnp.dot(h, wg_ref[...], preferred_element_type=F32)
    u = jnp.dot(h, wu_ref[...], preferred_element_type=F32)
    act = (a * _sigmoid(a) * u).astype(BF16)
    y = jnp.dot(act, wd_ref[...], preferred_element_type=F32)
    o_ref[...] = x + (0.5 * gate) * y


def _ffn(x, mod, layer, sub, g, wg, wu, wd, seq):
    t, d = x.shape
    dff = wg.shape[1]
    tm = TOKEN_TILE
    per_seq = seq // tm
    est = 3 * d * dff * 2 + 4 * tm * d * 4 + tm * dff * 14 + tm * d * 4
    return pl.pallas_call(
        functools.partial(_ffn_kernel, sub=sub),
        grid=(t // tm,),
        in_specs=[pl.BlockSpec((tm, d), lambda i: (i, 0)),
                  pl.BlockSpec((1, 1, N_SUB * 3, d), lambda i: (layer, i // per_seq, 0, 0)),
                  _resident((1, d)), _resident((d, dff)), _resident((d, dff)), _resident((dff, d))],
        out_specs=pl.BlockSpec((tm, d), lambda i: (i, 0)),
        out_shape=jax.ShapeDtypeStruct((t, d), F32),
        compiler_params=pltpu.CompilerParams(dimension_semantics=("arbitrary",),
                                             vmem_limit_bytes=_vmem_limit(est)),
        name="ffn",
    )(x, mod, g, wg, wu, wd)


def _inproj_kernel(x_ref, mod_ref, g_ref, w_ref, cw_ref, qkv1_ref, qkv2_ref, qkv3_ref, yc_ref, gate_ref,
                   pbuf, ubuf, *, per_seq, d):
    tm = x_ref.shape[0]
    shift, scale, _ = _mod_rows(mod_ref, 1)
    h = _norm_mod(x_ref[...], g_ref[...], shift, scale).astype(BF16)

    def proj(lo, width):
        return jnp.dot(h, w_ref[:, lo:lo + width], preferred_element_type=F32)

    outs = (qkv1_ref, qkv2_ref, qkv3_ref)
    for part in range(3):
        u = proj(part * QKV_WIDTH, QKV_WIDTH)
        if part == 0:
            u = u * (HEAD_DIM ** -0.5)
        col = part * GROUP_WIDTH
        for grp, (_, dil) in enumerate(DILATION_GROUPS):
            ug = u[:, grp * GROUP_WIDTH:(grp + 1) * GROUP_WIDTH]
            if dil == 1:
                outs[grp][:, col:col + GROUP_WIDTH] = ug.astype(BF16)
                continue
            run = tm // dil
            for s in range(SLABS):
                ubuf[s] = ug[:, s * LANES:(s + 1) * LANES]
            for s in range(SLABS):
                for r in range(dil):
                    outs[grp][r * run:(r + 1) * run, col + s * LANES:col + (s + 1) * LANES] = (
                        ubuf[s, pl.ds(r, run, stride=dil), :].astype(BF16))

    base = 3 * QKV_WIDTH
    cb = proj(base, d)
    p = proj(base + d, d) * proj(base + 2 * d, d)

    @pl.when(pl.program_id(0) % per_seq == 0)
    def _():
        pbuf[0:F32_SUBLANES, :] = jnp.zeros((F32_SUBLANES, d), F32)

    pbuf[F32_SUBLANES:F32_SUBLANES + tm, :] = p
    conv = (cw_ref[0:1, :] * pbuf[F32_SUBLANES - 2:F32_SUBLANES - 2 + tm, :]
            + cw_ref[1:2, :] * pbuf[F32_SUBLANES - 1:F32_SUBLANES - 1 + tm, :]
            + cw_ref[2:3, :] * p)
    yc_ref[...] = (cb * conv).astype(BF16)
    pbuf[0:F32_SUBLANES, :] = pbuf[tm:tm + F32_SUBLANES, :]

    gate_ref[...] = proj(base + 3 * d, 2 * d).astype(BF16)


def _inproj(x, mod, layer, g, w_in, conv_w, seq):
    t, d = x.shape
    n = w_in.shape[1]
    tm = TOKEN_TILE
    per_seq = seq // tm
    est = (d * n * 2 + 2 * tm * d * 4 + 2 * tm * (3 * QKV_WIDTH + 3 * d) * 2 + (tm + 8) * d * 4
           + SLABS * tm * LANES * 4 + tm * d * 4 * 8)
    row = lambda w: pl.BlockSpec((tm, w), lambda i: (i, 0))
    return pl.pallas_call(
        functools.partial(_inproj_kernel, per_seq=per_seq, d=d),
        grid=(t // tm,),
        in_specs=[row(d),
                  pl.BlockSpec((1, 1, N_SUB * 3, d), lambda i: (layer, i // per_seq, 0, 0)),
                  _resident((1, d)), _resident((d, n)), _resident((CONV_K, d))],
        out_specs=[row(QKV_WIDTH), row(QKV_WIDTH), row(QKV_WIDTH), row(d), row(2 * d)],
        out_shape=[jax.ShapeDtypeStruct((t, QKV_WIDTH), BF16)] * 3
                  + [jax.ShapeDtypeStruct((t, d), BF16), jax.ShapeDtypeStruct((t, 2 * d), BF16)],
        scratch_shapes=[pltpu.VMEM((tm + F32_SUBLANES, d), F32), pltpu.VMEM((SLABS, tm, LANES), F32)],
        compiler_params=pltpu.CompilerParams(dimension_semantics=("arbitrary",),
                                             vmem_limit_bytes=_vmem_limit(est)),
        name="inproj",
    )(x, mod, g, w_in, conv_w)


def _attn_kernel(q1, k1, v1, q2, k2, v2, q3, k3, v3, bias_ref, out_ref,
                 pk1, pv1, pk2, pv2, pk3, pv3, o1, l1, o2, l2, o3, l3):
    n = pl.program_id(2)
    ii = lax.broadcasted_iota(jnp.int32, (BLOCK, 2 * BLOCK), 0)
    jj = lax.broadcasted_iota(jnp.int32, (BLOCK, 2 * BLOCK), 1)
    band_prev = (jj >= ii) & (jj <= ii + BLOCK)
    band_first = (jj >= BLOCK) & (jj <= ii + BLOCK)
    lane = lax.broadcasted_iota(jnp.int32, (1, LANES), 1)
    low = lane < HEAD_DIM
    keeps = (low.astype(BF16), 1 - low.astype(BF16))
    nt = (((1,), (1,)), ((), ()))

    def unit(grp, q, k, v):
        band = band_prev
        if k.shape[0] == BLOCK:
            band = band_first
            zeros = jnp.zeros((BLOCK, LANES), BF16)
            k, v = jnp.concatenate([zeros, k], axis=0), jnp.concatenate([zeros, v], axis=0)
        v_ext = jnp.concatenate([v, jnp.ones((2 * BLOCK, LANES), BF16)], axis=1)
        parts = []
        for half in range(2):
            s = lax.dot_general(q * keeps[half], k, nt, preferred_element_type=F32)
            s = jnp.where(band, s + bias_ref[grp, half], NEG_INF)
            m = jnp.max(s, axis=-1, keepdims=True)
            p = jnp.exp(s - m).astype(BF16)
            oe = jnp.dot(p, v_ext, preferred_element_type=F32)
            parts.append((oe[:, :LANES], oe[:, LANES:], m))
        (oa, la, ma), (ob, lb, mb) = parts
        l = jnp.where(low, la, lb)
        return jnp.where(low, oa, ob) / l, jnp.where(low, ma, mb) + jnp.log(l)

    def span(start, size):
        return pl.ds(start if isinstance(start, int) else pl.multiple_of(start, 32), size)

    def rows(ref, start, size=BLOCK):
        return ref[span(start, size), :]

    def cat(a, b):
        return jnp.concatenate([a, b], axis=0)

    n_units = ATTN_TILE // BLOCK

    def g1_unit(u, with_prev=True):
        if isinstance(u, int) and u == 0:
            k, v = k1[0:BLOCK, :], v1[0:BLOCK, :]
            if with_prev:
                k, v = cat(pk1[...], k), cat(pv1[...], v)
            o, lse = unit(0, q1[0:BLOCK, :], k, v)
        else:
            start = u * BLOCK
            o, lse = unit(0, rows(q1, start), rows(k1, start - BLOCK, 2 * BLOCK),
                          rows(v1, start - BLOCK, 2 * BLOCK))
        dst = span(u * BLOCK, BLOCK)
        o1[dst, :] = o
        l1[dst, :] = lse

    def g1_head(with_prev):
        for u in range(ATTN_UNROLL):
            g1_unit(u, with_prev)

    def g1_rest(i, carry):
        for j in range(ATTN_UNROLL):
            g1_unit(i * ATTN_UNROLL + j)
        return carry

    d2 = DILATION_GROUPS[1][1]

    def g2_unit(n2, r, with_prev=True):
        cur = n2 * TOKEN_TILE + r * BLOCK
        k, v = rows(k2, cur), rows(v2, cur)
        if isinstance(n2, int) and n2 == 0:
            if with_prev:
                k, v = cat(pk2[r * BLOCK:(r + 1) * BLOCK, :], k), cat(pv2[r * BLOCK:(r + 1) * BLOCK, :], v)
        else:
            k, v = cat(rows(k2, cur - TOKEN_TILE), k), cat(rows(v2, cur - TOKEN_TILE), v)
        o, lse = unit(1, rows(q2, cur), k, v)
        o2[pl.ds(n2 * TOKEN_TILE + r, BLOCK, stride=d2), :] = o
        l2[pl.ds(n2 * TOKEN_TILE + r, BLOCK, stride=d2), :] = lse

    def g2_head(with_prev):
        for r in range(d2):
            g2_unit(0, r, with_prev)

    def g2_rest(n2, carry):
        for r in range(d2):
            g2_unit(n2, r)
        return carry

    d3 = DILATION_GROUPS[2][1]
    run3 = TOKEN_TILE // d3

    def gather3(ref, r):
        return jnp.concatenate([rows(ref, j * TOKEN_TILE + r * run3, run3) for j in range(ATTN_TILE // TOKEN_TILE)],
                               axis=0)

    def g3_body(with_prev):
        def body(i, carry):
            for j in range(ATTN_UNROLL):
                r = i * ATTN_UNROLL + j
                k, v = gather3(k3, r), gather3(v3, r)
                if with_prev:
                    k, v = cat(gather3(pk3, r), k), cat(gather3(pv3, r), v)
                o, lse = unit(2, gather3(q3, r), k, v)
                o3[pl.ds(r, BLOCK, stride=d3), :] = o
                l3[pl.ds(r, BLOCK, stride=d3), :] = lse
            return carry
        return body

    @pl.when(n == 0)
    def _():
        g1_head(False)
        g2_head(False)
        lax.fori_loop(0, d3 // ATTN_UNROLL, g3_body(False), 0)
        pk1[...] = k1[ATTN_TILE - BLOCK:, :]
        pv1[...] = v1[ATTN_TILE - BLOCK:, :]
        pk2[...] = k2[ATTN_TILE - TOKEN_TILE:, :]
        pv2[...] = v2[ATTN_TILE - TOKEN_TILE:, :]
        pk3[...] = k3[...]
        pv3[...] = v3[...]

    @pl.when(n != 0)
    def _():
        g1_head(True)
        g2_head(True)
        lax.fori_loop(0, d3 // ATTN_UNROLL, g3_body(True), 0)

    lax.fori_loop(1, n_units // ATTN_UNROLL, g1_rest, 0)
    lax.fori_loop(1, ATTN_TILE // TOKEN_TILE, g2_rest, 0)

    la, lb, lc = l1[...], l2[...], l3[...]
    m = jnp.maximum(jnp.maximum(la, lb), lc)
    ea, eb, ec = jnp.exp(la - m), jnp.exp(lb - m), jnp.exp(lc - m)
    mix = (ea * o1[...] + eb * o2[...] + ec * o3[...]) / (ea + eb + ec)
    out_ref[...] = mix.astype(BF16)


def _attention(qkvs, bias, batch, seq):
    t = qkvs[0].shape[0]
    assert seq == 2 * ATTN_TILE, "the previous-tile carry assumes two attention tiles per sequence"
    per_seq = seq // ATTN_TILE

    def part(p):
        return pl.BlockSpec((ATTN_TILE, LANES), lambda b, s, n: (b * per_seq + n, p * SLABS + s))

    in_specs = [part(p) for _ in range(N_GROUPS) for p in range(3)]
    in_specs.append(pl.BlockSpec((N_GROUPS, 2, BLOCK, 2 * BLOCK), lambda b, s, n: (0, s, 0, 0)))
    tails = [BLOCK, BLOCK, TOKEN_TILE, TOKEN_TILE, ATTN_TILE, ATTN_TILE]
    scratch = [pltpu.VMEM((rows, LANES), BF16) for rows in tails]
    scratch += [pltpu.VMEM((ATTN_TILE, LANES), F32)] * (2 * N_GROUPS)
    est = (2 * 9 * ATTN_TILE * LANES * 2 + 2 * N_GROUPS * 2 * BLOCK * 2 * BLOCK * 4 + sum(tails) * LANES * 2
           + 2 * N_GROUPS * ATTN_TILE * LANES * 4 + 8 * ATTN_TILE * LANES * 4)
    operands = [a for a in qkvs for _ in range(3)]
    return pl.pallas_call(
        _attn_kernel,
        grid=(batch, SLABS, per_seq),
        in_specs=in_specs,
        out_specs=pl.BlockSpec((ATTN_TILE, LANES), lambda b, s, n: (b * per_seq + n, s)),
        out_shape=jax.ShapeDtypeStruct((t, GROUP_WIDTH), BF16),
        scratch_shapes=scratch,
        compiler_params=pltpu.CompilerParams(dimension_semantics=("arbitrary", "arbitrary", "arbitrary"),
                                             vmem_limit_bytes=_vmem_limit(est)),
        name="attn",
    )(*operands, bias)


def _mixout_kernel(x_ref, mod_ref, mix_ref, yc_ref, gate_ref, wao_ref, wco_ref, wo_ref, out_ref, *, d):
    _, _, gate = _mod_rows(mod_ref, 1)
    y_attn = jnp.dot(mix_ref[...], wao_ref[...], preferred_element_type=F32)
    y_conv = jnp.dot(yc_ref[...], wco_ref[...], preferred_element_type=F32)
    g_conv = gate_ref[:, 0:d].astype(F32)
    g_attn = gate_ref[:, d:2 * d].astype(F32)
    merged = _sigmoid(g_conv) * y_conv + _sigmoid(g_attn) * y_attn
    y = jnp.dot(merged.astype(BF16), wo_ref[...], preferred_element_type=F32)
    out_ref[...] = x_ref[...] + gate * y


def _mixout(x, mod, layer, mix, yc, gates, wao, wco, wo, seq):
    t, d = x.shape
    tm = TOKEN_TILE
    per_seq = seq // tm
    row = lambda w: pl.BlockSpec((tm, w), lambda i: (i, 0))
    est = (2 * d * d + GROUP_WIDTH * d) * 2 + 2 * tm * (2 * d * 4 + GROUP_WIDTH * 2 + 3 * d * 2) + tm * d * 4 * 6
    return pl.pallas_call(
        functools.partial(_mixout_kernel, d=d),
        grid=(t // tm,),
        in_specs=[row(d), pl.BlockSpec((1, 1, N_SUB * 3, d), lambda i: (layer, i // per_seq, 0, 0)),
                  row(GROUP_WIDTH), row(d), row(2 * d),
                  _resident((GROUP_WIDTH, d)), _resident((d, d)), _resident((d, d))],
        out_specs=row(d),
        out_shape=jax.ShapeDtypeStruct((t, d), F32),
        compiler_params=pltpu.CompilerParams(dimension_semantics=("arbitrary",),
                                             vmem_limit_bytes=_vmem_limit(est)),
        name="mixout",
    )(x, mod, mix, yc, gates, wao, wco, wo)


def _final_kernel(x_ref, g_ref, o_ref):
    x = x_ref[...]
    ms = jnp.mean(x * x, axis=-1, keepdims=True)
    o_ref[...] = x * lax.rsqrt(ms + EPS) * g_ref[...]


def _final_norm(x, g):
    t, d = x.shape
    tm = TOKEN_TILE
    return pl.pallas_call(
        _final_kernel,
        grid=(t // tm,),
        in_specs=[pl.BlockSpec((tm, d), lambda i: (i, 0)), _resident((1, d))],
        out_specs=pl.BlockSpec((tm, d), lambda i: (i, 0)),
        out_shape=jax.ShapeDtypeStruct((t, d), F32),
        compiler_params=pltpu.CompilerParams(dimension_semantics=("arbitrary",)),
        name="final_norm",
    )(x, g)


def kernel(x, c, ada_w, ada_b, norm_g, ffn_w_gate, ffn_w_up, ffn_w_down, w_in, conv_w, w_conv_out, w_attn_out,
           w_o, rel_bias, final_g):
    batch, seq, d = x.shape
    depth = ada_w.shape[0]
    assert seq % ATTN_TILE == 0 and ATTN_TILE % TOKEN_TILE == 0
    assert all(win // dil == BLOCK and TOKEN_TILE % (dil * 32) == 0 or dil == 1 for win, dil in DILATION_GROUPS)
    assert w_in.shape[-1] == 3 * QKV_WIDTH + 5 * d and ada_w.shape[-1] == N_SUB * 3 * d

    mod = _ada_mod(c, ada_w, ada_b)
    bias = _t5_bias(rel_bias)
    xt = x.reshape(batch * seq, d)
    for l in range(depth):
        g = [norm_g[l, j].reshape(1, d) for j in range(N_SUB)]
        xt = _ffn(xt, mod, l, 0, g[0], ffn_w_gate[l, 0].astype(BF16), ffn_w_up[l, 0].astype(BF16),
                  ffn_w_down[l, 0].astype(BF16), seq)
        *qkvs, yc, gates = _inproj(xt, mod, l, g[1], w_in[l].astype(BF16), conv_w[l], seq)
        mix = _attention(qkvs, bias, batch, seq)
        xt = _mixout(xt, mod, l, mix, yc, gates, w_attn_out[l].astype(BF16),
                     w_conv_out[l].astype(BF16), w_o[l].astype(BF16), seq)
        xt = _ffn(xt, mod, l, 2, g[2], ffn_w_gate[l, 1].astype(BF16), ffn_w_up[l, 1].astype(BF16),
                  ffn_w_down[l, 1].astype(BF16), seq)
    return _final_norm(xt, final_g.reshape(1, d)).reshape(batch, seq, d)
```

```python
import functools

import numpy as np
import jax
import jax.numpy as jnp
from jax import lax
from jax.experimental import pallas as pl
from jax.experimental.pallas import tpu as pltpu

F32 = jnp.float32
BF16 = jnp.bfloat16

HEAD_DIM = 64
HEADS_PER_GROUP = 8
DILATION_GROUPS = ((128, 1), (512, 4), (2048, 16))
N_GROUPS = len(DILATION_GROUPS)
GROUP_WIDTH = HEADS_PER_GROUP * HEAD_DIM
QKV_WIDTH = N_GROUPS * GROUP_WIDTH
NUM_BUCKETS = 32
MAX_DISTANCE = 2048
BLOCK = 128
CONV_K = 3
N_SUB = 3
EPS = 1e-6
NEG_INF = -1e30

V7X_VMEM_BYTES = 64 * 1024 * 1024
LANES = 128
F32_SUBLANES = 8

TOKEN_TILE = 512
ADA_COL_TILE = 1152
MAX_DILATION = max(d for _, d in DILATION_GROUPS)
ATTN_TILE = BLOCK * MAX_DILATION
SLABS = GROUP_WIDTH // LANES
DEINTERLEAVE_STEP = 4


def _vmem_limit(nbytes):
    return int(min(nbytes * 5 // 4 + (4 << 20), V7X_VMEM_BYTES - (6 << 20)))


def _sigmoid(x):
    return 1.0 / (1.0 + jnp.exp(-x))


def _norm_mod(x, g, shift, scale):
    ms = jnp.mean(x * x, axis=-1, keepdims=True)
    y = x * lax.rsqrt(ms + EPS) * g
    return y * (1.0 + scale) + shift


def _mod_rows(mod_ref, sub):
    base = 3 * sub
    return (mod_ref[0, 0, base:base + 1, :], mod_ref[0, 0, base + 1:base + 2, :],
            mod_ref[0, 0, base + 2:base + 3, :])


def _resident(shape):
    return pl.BlockSpec(shape, lambda *_: (0,) * len(shape), pipeline_mode=pl.Buffered(1))


def _ada_kernel(c_ref, w_ref, b_ref, o_ref):
    c = c_ref[...]
    cs = (c * _sigmoid(c)).astype(BF16)
    o_ref[0] = jnp.dot(cs, w_ref[0].astype(BF16), preferred_element_type=F32) + b_ref[0]


def _ada_mod(c, ada_w, ada_b):
    depth, d, n = ada_w.shape
    b = c.shape[0]
    tn = ADA_COL_TILE
    out = pl.pallas_call(
        _ada_kernel,
        grid=(depth, n // tn),
        in_specs=[pl.BlockSpec((b, d), lambda l, j: (0, 0)),
                  pl.BlockSpec((1, d, tn), lambda l, j: (l, 0, j)),
                  pl.BlockSpec((1, 1, tn), lambda l, j: (l, 0, j))],
        out_specs=pl.BlockSpec((1, b, tn), lambda l, j: (l, 0, j)),
        out_shape=jax.ShapeDtypeStruct((depth, b, n), F32),
        compiler_params=pltpu.CompilerParams(
            dimension_semantics=("arbitrary", "arbitrary"),
            vmem_limit_bytes=_vmem_limit(2 * d * tn * 4 + d * tn * 2)),
        name="ada_mod",
    )(c, ada_w, ada_b.reshape(depth, 1, n))
    return out.reshape(depth, b, N_SUB * 3, d)


def _t5_bucket(dist):
    exact = NUM_BUCKETS // 2
    d = np.maximum(dist, 1).astype(np.float32)
    large = exact + (np.log(d / exact) / np.log(MAX_DISTANCE / exact) * (NUM_BUCKETS - exact)).astype(np.int32)
    large = np.minimum(large, NUM_BUCKETS - 1)
    return np.where(dist < exact, dist, large).astype(np.int32)


def _bucket_maps():
    i = np.arange(BLOCK)[:, None]
    j = np.arange(2 * BLOCK)[None, :]
    rel = np.maximum(i - j + BLOCK, 0)
    return np.stack([_t5_bucket(rel * dil) for _, dil in DILATION_GROUPS]).astype(np.int32)


def _bias_kernel(tab_ref, bucket_ref, o_ref):
    g = pl.program_id(0)
    bk = bucket_ref[0]
    for h in range(HEADS_PER_GROUP):
        acc = jnp.zeros(bk.shape, F32)
        for b in range(NUM_BUCKETS):
            acc = jnp.where(bk == b, tab_ref[b, g * HEADS_PER_GROUP + h], acc)
        o_ref[0, h] = acc


def _t5_bias(rel_bias):
    buckets = jnp.asarray(_bucket_maps())
    return pl.pallas_call(
        _bias_kernel,
        grid=(N_GROUPS,),
        in_specs=[pl.BlockSpec(memory_space=pltpu.SMEM),
                  pl.BlockSpec((1, BLOCK, 2 * BLOCK), lambda g: (g, 0, 0))],
        out_specs=pl.BlockSpec((1, HEADS_PER_GROUP, BLOCK, 2 * BLOCK), lambda g: (g, 0, 0, 0)),
        out_shape=jax.ShapeDtypeStruct((N_GROUPS, HEADS_PER_GROUP, BLOCK, 2 * BLOCK), F32),
        compiler_params=pltpu.CompilerParams(dimension_semantics=("arbitrary",)),
        name="t5_bias",
    )(rel_bias, buckets)


def _ffn_kernel(x_ref, mod_ref, g_ref, fg_ref, wg_ref, wu_ref, wd_ref, o_ref, *, sub, final):
    x = x_ref[...]
    shift, scale, gate = _mod_rows(mod_ref, sub)
    h = _norm_mod(x, g_ref[...], shift, scale).astype(BF16)
    a = jnp.dot(h, wg_ref[...], preferred_element_type=F32)
    u = jnp.dot(h, wu_ref[...], preferred_element_type=F32)
    act = (a * _sigmoid(a) * u).astype(BF16)
    y = jnp.dot(act, wd_ref[...], preferred_element_type=F32)
    out = x + (0.5 * gate) * y
    if final:
        ms = jnp.mean(out * out, axis=-1, keepdims=True)
        out = out * lax.rsqrt(ms + EPS) * fg_ref[...]
    o_ref[...] = out


def _ffn(x, mod, layer, sub, g, final_g, wg, wu, wd, seq, final=False):
    t, d = x.shape
    dff = wg.shape[1]
    tm = TOKEN_TILE
    per_seq = seq // tm
    est = 3 * d * dff * 2 + 4 * tm * d * 4 + tm * dff * 14 + tm * d * 4
    return pl.pallas_call(
        functools.partial(_ffn_kernel, sub=sub, final=final),
        grid=(t // tm,),
        in_specs=[pl.BlockSpec((tm, d), lambda i: (i, 0)),
                  pl.BlockSpec((1, 1, N_SUB * 3, d), lambda i: (layer, i // per_seq, 0, 0)),
                  _resident((1, d)), _resident((1, d)),
                  _resident((d, dff)), _resident((d, dff)), _resident((dff, d))],
        out_specs=pl.BlockSpec((tm, d), lambda i: (i, 0)),
        out_shape=jax.ShapeDtypeStruct((t, d), F32),
        compiler_params=pltpu.CompilerParams(dimension_semantics=("arbitrary",),
                                             vmem_limit_bytes=_vmem_limit(est)),
        name="ffn_final" if final else "ffn",
    )(x, mod, g, final_g, wg, wu, wd)


def _residue(grun, groups, r):
    low, weight, g = 0, 1, groups
    while g > 1:
        g //= DEINTERLEAVE_STEP
        low += ((grun // g) % DEINTERLEAVE_STEP) * weight
        weight *= DEINTERLEAVE_STEP
    return low + r * weight


def _inproj_kernel(x_ref, mod_ref, g_ref, w_ref, cw_ref, qkv1_ref, qkv2_ref, qkv3_ref, yc_ref,
                   gate_ref, pbuf, ubuf, tbuf, *, per_seq, d):
    tm = x_ref.shape[0]

    @pl.when(pl.program_id(0) % per_seq == 0)
    def _():
        pbuf[0:F32_SUBLANES, :] = jnp.zeros((F32_SUBLANES, d), F32)

    shift, scale, _ = _mod_rows(mod_ref, 1)
    h = _norm_mod(x_ref[...], g_ref[...], shift, scale).astype(BF16)

    def proj(lo, width):
        return jnp.dot(h, w_ref[:, lo:lo + width], preferred_element_type=F32)

    base = 3 * QKV_WIDTH
    p = proj(base + d, d) * proj(base + 2 * d, d)
    pbuf[F32_SUBLANES:F32_SUBLANES + tm, :] = p
    conv = (cw_ref[0:1, :] * pbuf[F32_SUBLANES - 2:F32_SUBLANES - 2 + tm, :]
            + cw_ref[1:2, :] * pbuf[F32_SUBLANES - 1:F32_SUBLANES - 1 + tm, :]
            + cw_ref[2:3, :] * p)
    yc_ref[...] = (proj(base, d) * conv).astype(BF16)
    pbuf[0:F32_SUBLANES, :] = pbuf[tm:tm + F32_SUBLANES, :]

    outs = (qkv1_ref, qkv2_ref, qkv3_ref)
    for part in range(3):
        u = proj(part * QKV_WIDTH, QKV_WIDTH)
        if part == 0:
            u = u * (HEAD_DIM ** -0.5)
        col = part * GROUP_WIDTH
        for grp, (_, dil) in enumerate(DILATION_GROUPS):
            ug = u[:, grp * GROUP_WIDTH:(grp + 1) * GROUP_WIDTH]
            if dil == 1:
                outs[grp][:, col:col + GROUP_WIDTH] = ug.astype(BF16)
                continue
            buf = ubuf.at[grp - 1]
            for s in range(SLABS):
                buf[s] = ug[:, s * LANES:(s + 1) * LANES]
            step, groups = 1, 1
            while step * DEINTERLEAVE_STEP < dil:
                assert buf is not tbuf, "one intermediate buffer: at most two passes"
                run = tm // (groups * DEINTERLEAVE_STEP)
                for s in range(SLABS):
                    for grun in range(groups):
                        for r in range(DEINTERLEAVE_STEP):
                            dst = (grun * DEINTERLEAVE_STEP + r) * run
                            tbuf[s, dst:dst + run, :] = (
                                buf[s, pl.ds(grun * (tm // groups) + r, run, stride=DEINTERLEAVE_STEP), :])
                buf, step, groups = tbuf, step * DEINTERLEAVE_STEP, groups * DEINTERLEAVE_STEP
            last = dil // step
            run = tm // dil
            for s in range(SLABS):
                for grun in range(groups):
                    for r in range(last):
                        res = _residue(grun, groups, r)
                        outs[grp][res * run:(res + 1) * run, col + s * LANES:col + (s + 1) * LANES] = (
                            buf[s, pl.ds(grun * (tm // groups) + r, run, stride=last), :].astype(BF16))

    for c in range(2):
        gate_ref[:, c * d:(c + 1) * d] = proj(base + 3 * d + c * d, d).astype(BF16)


def _inproj(x, mod, layer, g, w_in, conv_w, seq):
    t, d = x.shape
    n = w_in.shape[1]
    tm = TOKEN_TILE
    per_seq = seq // tm
    est = (d * n * 2 + 2 * tm * d * 4 + 2 * tm * (3 * QKV_WIDTH + 3 * d) * 2 + (tm + 8) * d * 4
           + N_GROUPS * SLABS * tm * LANES * 4 + tm * d * 4 * 8)
    row = lambda w: pl.BlockSpec((tm, w), lambda i: (i, 0))
    return pl.pallas_call(
        functools.partial(_inproj_kernel, per_seq=per_seq, d=d),
        grid=(t // tm,),
        in_specs=[row(d),
                  pl.BlockSpec((1, 1, N_SUB * 3, d), lambda i: (layer, i // per_seq, 0, 0)),
                  _resident((1, d)), _resident((d, n)), _resident((CONV_K, d))],
        out_specs=[row(QKV_WIDTH), row(QKV_WIDTH), row(QKV_WIDTH), row(d), row(2 * d)],
        out_shape=[jax.ShapeDtypeStruct((t, QKV_WIDTH), BF16)] * 3
                  + [jax.ShapeDtypeStruct((t, d), BF16), jax.ShapeDtypeStruct((t, 2 * d), BF16)],
        scratch_shapes=[pltpu.VMEM((tm + F32_SUBLANES, d), F32),
                        pltpu.VMEM((N_GROUPS - 1, SLABS, tm, LANES), F32),
                        pltpu.VMEM((SLABS, tm, LANES), F32)],
        compiler_params=pltpu.CompilerParams(dimension_semantics=("arbitrary",),
                                             vmem_limit_bytes=_vmem_limit(est)),
        name="inproj",
    )(x, mod, g, w_in, conv_w)


def _attn_kernel(q1, k1, v1, q2, k2, v2, q3, k3, v3, bias_ref, out_ref,
                 pk1, pv1, pk2, pv2, pk3, pv3, o1, l1, o2, l2, o3, l3, cap_ref):
    n = pl.program_id(2)
    ii = lax.broadcasted_iota(jnp.int32, (BLOCK, 2 * BLOCK), 0)
    jj = lax.broadcasted_iota(jnp.int32, (BLOCK, 2 * BLOCK), 1)
    in_band = jj <= ii + BLOCK
    cap_ref[0] = jnp.where((jj >= ii) & in_band, jnp.inf, NEG_INF)
    cap_ref[1] = jnp.where((jj >= BLOCK) & in_band, jnp.inf, NEG_INF)
    lane = lax.broadcasted_iota(jnp.int32, (1, LANES), 1)
    low = lane < HEAD_DIM
    keeps = (low.astype(BF16), 1 - low.astype(BF16))
    nt = (((1,), (1,)), ((), ()))

    def unit(grp, q, k, v, cap):
        v_ext = jnp.concatenate([v, jnp.ones((2 * BLOCK, LANES), BF16)], axis=1)
        parts = []
        for half in range(2):
            s = lax.dot_general(q * keeps[half], k, nt, preferred_element_type=F32)
            s = jnp.minimum(s + bias_ref[grp, half], cap)
            m = jnp.max(s, axis=-1, keepdims=True)
            p = jnp.exp(s - m).astype(BF16)
            oe = jnp.dot(p, v_ext, preferred_element_type=F32)
            parts.append((oe[:, :LANES], oe[:, LANES:], m))
        (oa, la, ma), (ob, lb, mb) = parts
        l = jnp.where(low, la, lb)
        return jnp.where(low, oa, ob) / l, jnp.where(low, ma, mb) + jnp.log(l)

    def cat(a, b):
        return jnp.concatenate([a, b], axis=0)

    first = n == 0
    tail_cap = cap_ref[first.astype(jnp.int32)]
    full_cap = cap_ref[0]

    @pl.when(first)
    def _():
        for ref in (pk1, pv1, pk2, pv2, pk3, pv3):
            ref[...] = jnp.zeros(ref.shape, BF16)

    for u in range(ATTN_TILE // BLOCK):
        cur = slice(u * BLOCK, (u + 1) * BLOCK)
        if u == 0:
            k, v, cap = cat(pk1[...], k1[cur, :]), cat(pv1[...], v1[cur, :]), tail_cap
        else:
            both = slice((u - 1) * BLOCK, (u + 1) * BLOCK)
            k, v, cap = k1[both, :], v1[both, :], full_cap
        o1[cur, :], l1[cur, :] = unit(0, q1[cur, :], k, v, cap)

    d2 = DILATION_GROUPS[1][1]
    for n2 in range(ATTN_TILE // TOKEN_TILE):
        for r in range(d2):
            cur = slice(n2 * TOKEN_TILE + r * BLOCK, n2 * TOKEN_TILE + (r + 1) * BLOCK)
            if n2 == 0:
                run = slice(r * BLOCK, (r + 1) * BLOCK)
                k, v, cap = cat(pk2[run, :], k2[cur, :]), cat(pv2[run, :], v2[cur, :]), tail_cap
            else:
                prev = slice(cur.start - TOKEN_TILE, cur.stop - TOKEN_TILE)
                k, v, cap = cat(k2[prev, :], k2[cur, :]), cat(v2[prev, :], v2[cur, :]), full_cap
            o, lse = unit(1, q2[cur, :], k, v, cap)
            o2[pl.ds(n2 * TOKEN_TILE + r, BLOCK, stride=d2), :] = o
            l2[pl.ds(n2 * TOKEN_TILE + r, BLOCK, stride=d2), :] = lse

    d3 = DILATION_GROUPS[2][1]
    run3 = TOKEN_TILE // d3

    def gather3(ref, r):
        return jnp.concatenate([ref[j * TOKEN_TILE + r * run3:j * TOKEN_TILE + (r + 1) * run3, :]
                                for j in range(ATTN_TILE // TOKEN_TILE)], axis=0)

    for r in range(d3):
        o, lse = unit(2, gather3(q3, r), cat(gather3(pk3, r), gather3(k3, r)),
                      cat(gather3(pv3, r), gather3(v3, r)), tail_cap)
        o3[pl.ds(r, BLOCK, stride=d3), :] = o
        l3[pl.ds(r, BLOCK, stride=d3), :] = lse

    @pl.when(first)
    def _():
        pk1[...] = k1[ATTN_TILE - BLOCK:, :]
        pv1[...] = v1[ATTN_TILE - BLOCK:, :]
        pk2[...] = k2[ATTN_TILE - TOKEN_TILE:, :]
        pv2[...] = v2[ATTN_TILE - TOKEN_TILE:, :]
        pk3[...] = k3[...]
        pv3[...] = v3[...]

    la, lb, lc = l1[...], l2[...], l3[...]
    m = jnp.maximum(jnp.maximum(la, lb), lc)
    ea, eb, ec = jnp.exp(la - m), jnp.exp(lb - m), jnp.exp(lc - m)
    mix = (ea * o1[...] + eb * o2[...] + ec * o3[...]) / (ea + eb + ec)
    out_ref[...] = mix.astype(BF16)


def _attention(qkvs, bias, batch, seq):
    t = qkvs[0].shape[0]
    assert seq == 2 * ATTN_TILE, "the previous-tile carry assumes two attention tiles per sequence"
    per_seq = seq // ATTN_TILE

    def part(p):
        return pl.BlockSpec((ATTN_TILE, LANES), lambda b, s, n: (b * per_seq + n, p * SLABS + s))

    in_specs = [part(p) for _ in range(N_GROUPS) for p in range(3)]
    in_specs.append(pl.BlockSpec((N_GROUPS, 2, BLOCK, 2 * BLOCK), lambda b, s, n: (0, s, 0, 0)))
    tails = [BLOCK, BLOCK, TOKEN_TILE, TOKEN_TILE, ATTN_TILE, ATTN_TILE]
    scratch = [pltpu.VMEM((rows, LANES), BF16) for rows in tails]
    scratch += [pltpu.VMEM((ATTN_TILE, LANES), F32)] * (2 * N_GROUPS)
    scratch += [pltpu.VMEM((2, BLOCK, 2 * BLOCK), F32)]
    est = (2 * 9 * ATTN_TILE * LANES * 2 + 2 * N_GROUPS * 2 * BLOCK * 2 * BLOCK * 4 + sum(tails) * LANES * 2
           + 2 * N_GROUPS * ATTN_TILE * LANES * 4 + 8 * ATTN_TILE * LANES * 4)
    operands = [a for a in qkvs for _ in range(3)]
    return pl.pallas_call(
        _attn_kernel,
        grid=(batch, SLABS, per_seq),
        in_specs=in_specs,
        out_specs=pl.BlockSpec((ATTN_TILE, LANES), lambda b, s, n: (b * per_seq + n, s)),
        out_shape=jax.ShapeDtypeStruct((t, GROUP_WIDTH), BF16),
        scratch_shapes=scratch,
        compiler_params=pltpu.CompilerParams(dimension_semantics=("arbitrary", "arbitrary", "arbitrary"),
                                             vmem_limit_bytes=_vmem_limit(est)),
        name="attn",
    )(*operands, bias)


def _mixout_kernel(x_ref, mod_ref, mix_ref, yc_ref, gate_ref, wao_ref, wco_ref, wo_ref, out_ref, *, d):
    _, _, gate = _mod_rows(mod_ref, 1)
    y_attn = jnp.dot(mix_ref[...], wao_ref[...], preferred_element_type=F32)
    y_conv = jnp.dot(yc_ref[...], wco_ref[...], preferred_element_type=F32)
    g_conv = gate_ref[:, 0:d].astype(F32)
    g_attn = gate_ref[:, d:2 * d].astype(F32)
    merged = _sigmoid(g_conv) * y_conv + _sigmoid(g_attn) * y_attn
    y = jnp.dot(merged.astype(BF16), wo_ref[...], preferred_element_type=F32)
    out_ref[...] = x_ref[...] + gate * y


def _mixout(x, mod, layer, mix, yc, gates, wao, wco, wo, seq):
    t, d = x.shape
    tm = TOKEN_TILE
    per_seq = seq // tm
    row = lambda w: pl.BlockSpec((tm, w), lambda i: (i, 0))
    est = (2 * d * d + GROUP_WIDTH * d) * 2 + 2 * tm * (2 * d * 4 + GROUP_WIDTH * 2 + 3 * d * 2) + tm * d * 4 * 6
    return pl.pallas_call(
        functools.partial(_mixout_kernel, d=d),
        grid=(t // tm,),
        in_specs=[row(d), pl.BlockSpec((1, 1, N_SUB * 3, d), lambda i: (layer, i // per_seq, 0, 0)),
                  row(GROUP_WIDTH), row(d), row(2 * d),
                  _resident((GROUP_WIDTH, d)), _resident((d, d)), _resident((d, d))],
        out_specs=row(d),
        out_shape=jax.ShapeDtypeStruct((t, d), F32),
        compiler_params=pltpu.CompilerParams(dimension_semantics=("arbitrary",),
                                             vmem_limit_bytes=_vmem_limit(est)),
        name="mixout",
    )(x, mod, mix, yc, gates, wao, wco, wo)


def kernel(x, c, ada_w, ada_b, norm_g, ffn_w_gate, ffn_w_up, ffn_w_down, w_in, conv_w, w_conv_out, w_attn_out,
           w_o, rel_bias, final_g):
    batch, seq, d = x.shape
    depth = ada_w.shape[0]
    assert seq % ATTN_TILE == 0 and ATTN_TILE % TOKEN_TILE == 0
    assert all(win // dil == BLOCK and TOKEN_TILE % (dil * 32) == 0 or dil == 1 for win, dil in DILATION_GROUPS)
    assert w_in.shape[-1] == 3 * QKV_WIDTH + 5 * d and ada_w.shape[-1] == N_SUB * 3 * d

    mod = _ada_mod(c, ada_w, ada_b)
    bias = _t5_bias(rel_bias)
    xt = x.reshape(batch * seq, d)
    fg = final_g.reshape(1, d)
    for l in range(depth):
        g = [norm_g[l, j].reshape(1, d) for j in range(N_SUB)]
        xt = _ffn(xt, mod, l, 0, g[0], fg, ffn_w_gate[l, 0].astype(BF16), ffn_w_up[l, 0].astype(BF16),
                  ffn_w_down[l, 0].astype(BF16), seq)
        *qkvs, yc, gates = _inproj(xt, mod, l, g[1], w_in[l].astype(BF16), conv_w[l], seq)
        mix = _attention(qkvs, bias, batch, seq)
        xt = _mixout(xt, mod, l, mix, yc, gates, w_attn_out[l].astype(BF16),
                     w_conv_out[l].astype(BF16), w_o[l].astype(BF16), seq)
        xt = _ffn(xt, mod, l, 2, g[2], fg, ffn_w_gate[l, 1].astype(BF16), ffn_w_up[l, 1].astype(BF16),
                  ffn_w_down[l, 1].astype(BF16), seq, final=l == depth - 1)
    return xt.reshape(batch, seq, d)
```

```python
import functools

import numpy as np
import jax
import jax.numpy as jnp
from jax import lax
from jax.experimental import pallas as pl
from jax.experimental.pallas import tpu as pltpu

F32 = jnp.float32
BF16 = jnp.bfloat16

HEAD_DIM = 64
HEADS_PER_GROUP = 8
DILATION_GROUPS = ((128, 1), (512, 4), (2048, 16))
N_GROUPS = len(DILATION_GROUPS)
GROUP_WIDTH = HEADS_PER_GROUP * HEAD_DIM
QKV_WIDTH = N_GROUPS * GROUP_WIDTH
NUM_BUCKETS = 32
MAX_DISTANCE = 2048
BLOCK = 128
CONV_K = 3
N_SUB = 3
EPS = 1e-6
NEG_INF = -1e30

V7X_VMEM_BYTES = 64 * 1024 * 1024
LANES = 128
MXU_TILE = 256
F32_SUBLANES = 8

FFN_TILE = 1024
FFN_CHUNKS = 2
TOKEN_TILE = 512
ADA_COL_TILE = 1152
MAX_DILATION = max(d for _, d in DILATION_GROUPS)
ATTN_TILE = BLOCK * MAX_DILATION
SLABS = GROUP_WIDTH // LANES
DEINTERLEAVE_STEP = 4


def _vmem_limit(nbytes):
    return int(min(nbytes * 5 // 4 + (4 << 20), V7X_VMEM_BYTES - (6 << 20)))


def _sigmoid(x):
    return 1.0 / (1.0 + jnp.exp(-x))


def _norm_mod(x, g, shift, scale):
    ms = jnp.mean(x * x, axis=-1, keepdims=True)
    y = x * lax.rsqrt(ms + EPS) * g
    return y * (1.0 + scale) + shift


def _mod_rows(mod_ref, sub):
    base = 3 * sub
    return (mod_ref[0, 0, base:base + 1, :], mod_ref[0, 0, base + 1:base + 2, :],
            mod_ref[0, 0, base + 2:base + 3, :])


def _resident(shape, lead=()):
    index = tuple(lead) + (0,) * len(shape)
    return pl.BlockSpec((None,) * len(lead) + tuple(shape), lambda *_: index, pipeline_mode=pl.Buffered(1))


def _ada_kernel(c_ref, w_ref, b_ref, o_ref):
    c = c_ref[...]
    cs = (c * _sigmoid(c)).astype(BF16)
    o_ref[0] = jnp.dot(cs, w_ref[0].astype(BF16), preferred_element_type=F32) + b_ref[0]


def _ada_mod(c, ada_w, ada_b):
    depth, d, n = ada_w.shape
    b = c.shape[0]
    tn = ADA_COL_TILE
    out = pl.pallas_call(
        _ada_kernel,
        grid=(depth, n // tn),
        in_specs=[pl.BlockSpec((b, d), lambda l, j: (0, 0)),
                  pl.BlockSpec((1, d, tn), lambda l, j: (l, 0, j)),
                  pl.BlockSpec((1, 1, tn), lambda l, j: (l, 0, j))],
        out_specs=pl.BlockSpec((1, b, tn), lambda l, j: (l, 0, j)),
        out_shape=jax.ShapeDtypeStruct((depth, b, n), F32),
        compiler_params=pltpu.CompilerParams(
            dimension_semantics=("arbitrary", "arbitrary"),
            vmem_limit_bytes=_vmem_limit(2 * d * tn * 4 + d * tn * 2)),
        name="ada_mod",
    )(c, ada_w, ada_b.reshape(depth, 1, n))
    return out.reshape(depth, b, N_SUB * 3, d)


def _t5_bucket(dist):
    exact = NUM_BUCKETS // 2
    d = np.maximum(dist, 1).astype(np.float32)
    large = exact + (np.log(d / exact) / np.log(MAX_DISTANCE / exact) * (NUM_BUCKETS - exact)).astype(np.int32)
    large = np.minimum(large, NUM_BUCKETS - 1)
    return np.where(dist < exact, dist, large).astype(np.int32)


def _bucket_maps():
    i = np.arange(BLOCK)[:, None]
    j = np.arange(2 * BLOCK)[None, :]
    rel = np.maximum(i - j + BLOCK, 0)
    return np.stack([_t5_bucket(rel * dil) for _, dil in DILATION_GROUPS]).astype(np.int32)


def _bias_kernel(tab_ref, bucket_ref, o_ref):
    g = pl.program_id(0)
    bk = bucket_ref[0]
    for h in range(HEADS_PER_GROUP):
        acc = jnp.zeros(bk.shape, F32)
        for b in range(NUM_BUCKETS):
            acc = jnp.where(bk == b, tab_ref[b, g * HEADS_PER_GROUP + h], acc)
        o_ref[0, h] = acc


def _t5_bias(rel_bias):
    buckets = jnp.asarray(_bucket_maps())
    return pl.pallas_call(
        _bias_kernel,
        grid=(N_GROUPS,),
        in_specs=[pl.BlockSpec(memory_space=pltpu.SMEM),
                  pl.BlockSpec((1, BLOCK, 2 * BLOCK), lambda g: (g, 0, 0))],
        out_specs=pl.BlockSpec((1, HEADS_PER_GROUP, BLOCK, 2 * BLOCK), lambda g: (g, 0, 0, 0)),
        out_shape=jax.ShapeDtypeStruct((N_GROUPS, HEADS_PER_GROUP, BLOCK, 2 * BLOCK), F32),
        compiler_params=pltpu.CompilerParams(dimension_semantics=("arbitrary",)),
        name="t5_bias",
    )(rel_bias, buckets)


def _ff_chunks(dff):
    tiles = dff // MXU_TILE
    assert tiles * MXU_TILE == dff
    sizes = [tiles // FFN_CHUNKS + (1 if c < tiles % FFN_CHUNKS else 0) for c in range(FFN_CHUNKS)]
    edges = [0]
    for sz in sizes:
        edges.append(edges[-1] + sz * MXU_TILE)
    return list(zip(edges[:-1], edges[1:]))


def _ffn_kernel(x_ref, mod_ref, g_ref, fg_ref, wg_ref, wu_ref, wd_ref, o_ref, *, sub, final):
    x = x_ref[...]
    shift, scale, gate = _mod_rows(mod_ref, sub)
    h = _norm_mod(x, g_ref[...], shift, scale).astype(BF16)
    y = None
    for lo, hi in _ff_chunks(wg_ref.shape[1]):
        a = jnp.dot(h, wg_ref[:, lo:hi], preferred_element_type=F32)
        u = jnp.dot(h, wu_ref[:, lo:hi], preferred_element_type=F32)
        act = (a * _sigmoid(a) * u).astype(BF16)
        part = jnp.dot(act, wd_ref[lo:hi, :], preferred_element_type=F32)
        y = part if y is None else y + part
    out = x + (0.5 * gate) * y
    if final:
        ms = jnp.mean(out * out, axis=-1, keepdims=True)
        out = out * lax.rsqrt(ms + EPS) * fg_ref[...]
    o_ref[...] = out


def _ffn(x, mod, layer, sub, g, final_g, wg, wu, wd, seq, final=False):
    t, d = x.shape
    dff = wg.shape[-1]
    which = (layer, sub // 2)
    tm = FFN_TILE
    per_seq = seq // tm
    est = 3 * d * dff * 2 + 4 * tm * d * 4 + tm * (dff // FFN_CHUNKS + MXU_TILE) * 10 + tm * d * 10
    return pl.pallas_call(
        functools.partial(_ffn_kernel, sub=sub, final=final),
        grid=(t // tm,),
        in_specs=[pl.BlockSpec((tm, d), lambda i: (i, 0)),
                  pl.BlockSpec((1, 1, N_SUB * 3, d), lambda i: (layer, i // per_seq, 0, 0)),
                  _resident((1, d)), _resident((1, d)),
                  _resident((d, dff), which), _resident((d, dff), which), _resident((dff, d), which)],
        out_specs=pl.BlockSpec((tm, d), lambda i: (i, 0)),
        out_shape=jax.ShapeDtypeStruct((t, d), F32),
        compiler_params=pltpu.CompilerParams(dimension_semantics=("arbitrary",),
                                             vmem_limit_bytes=_vmem_limit(est)),
        name="ffn_final" if final else "ffn",
    )(x, mod, g, final_g, wg, wu, wd)


def _residue(grun, groups, r):
    low, weight, g = 0, 1, groups
    while g > 1:
        g //= DEINTERLEAVE_STEP
        low += ((grun // g) % DEINTERLEAVE_STEP) * weight
        weight *= DEINTERLEAVE_STEP
    return low + r * weight


def _inproj_kernel(x_ref, mod_ref, g_ref, w_ref, cw_ref, qkv1_ref, qkv2_ref, qkv3_ref, yc_ref,
                   gate_ref, pbuf, ubuf, tbuf, *, per_seq, d):
    tm = x_ref.shape[0]

    @pl.when(pl.program_id(0) % per_seq == 0)
    def _():
        pbuf[0:F32_SUBLANES, :] = jnp.zeros((F32_SUBLANES, d), F32)

    shift, scale, _ = _mod_rows(mod_ref, 1)
    h = _norm_mod(x_ref[...], g_ref[...], shift, scale).astype(BF16)

    def proj(lo, width):
        return jnp.dot(h, w_ref[:, lo:lo + width], preferred_element_type=F32)

    base = 3 * QKV_WIDTH
    p = proj(base + d, d) * proj(base + 2 * d, d)
    pbuf[F32_SUBLANES:F32_SUBLANES + tm, :] = p
    conv = (cw_ref[0:1, :] * pbuf[F32_SUBLANES - 2:F32_SUBLANES - 2 + tm, :]
            + cw_ref[1:2, :] * pbuf[F32_SUBLANES - 1:F32_SUBLANES - 1 + tm, :]
            + cw_ref[2:3, :] * p)
    yc_ref[...] = (proj(base, d) * conv).astype(BF16)
    pbuf[0:F32_SUBLANES, :] = pbuf[tm:tm + F32_SUBLANES, :]

    outs = (qkv1_ref, qkv2_ref, qkv3_ref)
    for part in range(3):
        u = proj(part * QKV_WIDTH, QKV_WIDTH)
        if part == 0:
            u = u * (HEAD_DIM ** -0.5)
        col = part * GROUP_WIDTH
        for grp, (_, dil) in enumerate(DILATION_GROUPS):
            ug = u[:, grp * GROUP_WIDTH:(grp + 1) * GROUP_WIDTH]
            if dil == 1:
                outs[grp][:, col:col + GROUP_WIDTH] = ug.astype(BF16)
                continue
            buf = ubuf.at[grp - 1]
            for s in range(SLABS):
                buf[s] = ug[:, s * LANES:(s + 1) * LANES]
            step, groups = 1, 1
            while step * DEINTERLEAVE_STEP < dil:
                assert buf is not tbuf, "one intermediate buffer: at most two passes"
                run = tm // (groups * DEINTERLEAVE_STEP)
                for s in range(SLABS):
                    for grun in range(groups):
                        for r in range(DEINTERLEAVE_STEP):
                            dst = (grun * DEINTERLEAVE_STEP + r) * run
                            tbuf[s, dst:dst + run, :] = (
                                buf[s, pl.ds(grun * (tm // groups) + r, run, stride=DEINTERLEAVE_STEP), :])
                buf, step, groups = tbuf, step * DEINTERLEAVE_STEP, groups * DEINTERLEAVE_STEP
            last = dil // step
            run = tm // dil
            for s in range(SLABS):
                for grun in range(groups):
                    for r in range(last):
                        res = _residue(grun, groups, r)
                        outs[grp][res * run:(res + 1) * run, col + s * LANES:col + (s + 1) * LANES] = (
                            buf[s, pl.ds(grun * (tm // groups) + r, run, stride=last), :].astype(BF16))

    for c in range(2):
        gate_ref[:, c * d:(c + 1) * d] = proj(base + 3 * d + c * d, d).astype(BF16)


def _inproj(x, mod, layer, g, w_in, conv_w, seq):
    t, d = x.shape
    n = w_in.shape[-1]
    tm = TOKEN_TILE
    per_seq = seq // tm
    est = (d * n * 2 + 2 * tm * d * 4 + 2 * tm * (3 * QKV_WIDTH + 3 * d) * 2 + (tm + 8) * d * 4
           + N_GROUPS * SLABS * tm * LANES * 4 + tm * d * 4 * 8)
    row = lambda w: pl.BlockSpec((tm, w), lambda i: (i, 0))
    return pl.pallas_call(
        functools.partial(_inproj_kernel, per_seq=per_seq, d=d),
        grid=(t // tm,),
        in_specs=[row(d),
                  pl.BlockSpec((1, 1, N_SUB * 3, d), lambda i: (layer, i // per_seq, 0, 0)),
                  _resident((1, d)), _resident((d, n), (layer,)), _resident((CONV_K, d), (layer,))],
        out_specs=[row(QKV_WIDTH), row(QKV_WIDTH), row(QKV_WIDTH), row(d), row(2 * d)],
        out_shape=[jax.ShapeDtypeStruct((t, QKV_WIDTH), BF16)] * 3
                  + [jax.ShapeDtypeStruct((t, d), BF16), jax.ShapeDtypeStruct((t, 2 * d), BF16)],
        scratch_shapes=[pltpu.VMEM((tm + F32_SUBLANES, d), F32),
                        pltpu.VMEM((N_GROUPS - 1, SLABS, tm, LANES), F32),
                        pltpu.VMEM((SLABS, tm, LANES), F32)],
        compiler_params=pltpu.CompilerParams(dimension_semantics=("arbitrary",),
                                             vmem_limit_bytes=_vmem_limit(est)),
        name="inproj",
    )(x, mod, g, w_in, conv_w)


def _attn_kernel(q1, k1, v1, q2, k2, v2, q3, k3, v3, bias_ref, out_ref,
                 pk1, pv1, pk2, pv2, pk3, pv3, o1, l1, o2, l2, o3, l3, cap_ref):
    n = pl.program_id(2)
    ii = lax.broadcasted_iota(jnp.int32, (BLOCK, 2 * BLOCK), 0)
    jj = lax.broadcasted_iota(jnp.int32, (BLOCK, 2 * BLOCK), 1)
    in_band = jj <= ii + BLOCK
    cap_ref[0] = jnp.where((jj >= ii) & in_band, jnp.inf, NEG_INF)
    cap_ref[1] = jnp.where((jj >= BLOCK) & in_band, jnp.inf, NEG_INF)
    lane = lax.broadcasted_iota(jnp.int32, (1, LANES), 1)
    low = lane < HEAD_DIM
    keeps = (low.astype(BF16), 1 - low.astype(BF16))
    nt = (((1,), (1,)), ((), ()))

    def unit(grp, q, k, v, cap):
        v_ext = jnp.concatenate([v, jnp.ones((2 * BLOCK, LANES), BF16)], axis=1)
        parts = []
        for half in range(2):
            s = lax.dot_general(q * keeps[half], k, nt, preferred_element_type=F32)
            s = jnp.minimum(s + bias_ref[grp, half], cap)
            m = jnp.max(s, axis=-1, keepdims=True)
            p = jnp.exp(s - m).astype(BF16)
            oe = jnp.dot(p, v_ext, preferred_element_type=F32)
            parts.append((oe[:, :LANES], oe[:, LANES:], m))
        (oa, la, ma), (ob, lb, mb) = parts
        l = jnp.where(low, la, lb)
        return jnp.where(low, oa, ob) / l, jnp.where(low, ma, mb) + jnp.log(l)

    def cat(a, b):
        return jnp.concatenate([a, b], axis=0)

    first = n == 0
    tail_cap = cap_ref[first.astype(jnp.int32)]
    full_cap = cap_ref[0]

    @pl.when(first)
    def _():
        for ref in (pk1, pv1, pk2, pv2, pk3, pv3):
            ref[...] = jnp.zeros(ref.shape, BF16)

    for u in range(ATTN_TILE // BLOCK):
        cur = slice(u * BLOCK, (u + 1) * BLOCK)
        if u == 0:
            k, v, cap = cat(pk1[...], k1[cur, :]), cat(pv1[...], v1[cur, :]), tail_cap
        else:
            both = slice((u - 1) * BLOCK, (u + 1) * BLOCK)
            k, v, cap = k1[both, :], v1[both, :], full_cap
        o1[cur, :], l1[cur, :] = unit(0, q1[cur, :], k, v, cap)

    d2 = DILATION_GROUPS[1][1]
    for n2 in range(ATTN_TILE // TOKEN_TILE):
        for r in range(d2):
            cur = slice(n2 * TOKEN_TILE + r * BLOCK, n2 * TOKEN_TILE + (r + 1) * BLOCK)
            if n2 == 0:
                run = slice(r * BLOCK, (r + 1) * BLOCK)
                k, v, cap = cat(pk2[run, :], k2[cur, :]), cat(pv2[run, :], v2[cur, :]), tail_cap
            else:
                prev = slice(cur.start - TOKEN_TILE, cur.stop - TOKEN_TILE)
                k, v, cap = cat(k2[prev, :], k2[cur, :]), cat(v2[prev, :], v2[cur, :]), full_cap
            o, lse = unit(1, q2[cur, :], k, v, cap)
            o2[pl.ds(n2 * TOKEN_TILE + r, BLOCK, stride=d2), :] = o
            l2[pl.ds(n2 * TOKEN_TILE + r, BLOCK, stride=d2), :] = lse

    d3 = DILATION_GROUPS[2][1]
    run3 = TOKEN_TILE // d3

    def gather3(ref, r):
        return jnp.concatenate([ref[j * TOKEN_TILE + r * run3:j * TOKEN_TILE + (r + 1) * run3, :]
                                for j in range(ATTN_TILE // TOKEN_TILE)], axis=0)

    for r in range(d3):
        o, lse = unit(2, gather3(q3, r), cat(gather3(pk3, r), gather3(k3, r)),
                      cat(gather3(pv3, r), gather3(v3, r)), tail_cap)
        o3[pl.ds(r, BLOCK, stride=d3), :] = o
        l3[pl.ds(r, BLOCK, stride=d3), :] = lse

    @pl.when(first)
    def _():
        pk1[...] = k1[ATTN_TILE - BLOCK:, :]
        pv1[...] = v1[ATTN_TILE - BLOCK:, :]
        pk2[...] = k2[ATTN_TILE - TOKEN_TILE:, :]
        pv2[...] = v2[ATTN_TILE - TOKEN_TILE:, :]
        pk3[...] = k3[...]
        pv3[...] = v3[...]

    la, lb, lc = l1[...], l2[...], l3[...]
    m = jnp.maximum(jnp.maximum(la, lb), lc)
    ea, eb, ec = jnp.exp(la - m), jnp.exp(lb - m), jnp.exp(lc - m)
    mix = (ea * o1[...] + eb * o2[...] + ec * o3[...]) / (ea + eb + ec)
    out_ref[...] = mix.astype(BF16)


def _attention(qkvs, bias, batch, seq):
    t = qkvs[0].shape[0]
    assert seq == 2 * ATTN_TILE, "the previous-tile carry assumes two attention tiles per sequence"
    per_seq = seq // ATTN_TILE

    def part(p):
        return pl.BlockSpec((ATTN_TILE, LANES), lambda b, s, n: (b * per_seq + n, p * SLABS + s))

    in_specs = [part(p) for _ in range(N_GROUPS) for p in range(3)]
    in_specs.append(pl.BlockSpec((N_GROUPS, 2, BLOCK, 2 * BLOCK), lambda b, s, n: (0, s, 0, 0)))
    tails = [BLOCK, BLOCK, TOKEN_TILE, TOKEN_TILE, ATTN_TILE, ATTN_TILE]
    scratch = [pltpu.VMEM((rows, LANES), BF16) for rows in tails]
    scratch += [pltpu.VMEM((ATTN_TILE, LANES), F32)] * (2 * N_GROUPS)
    scratch += [pltpu.VMEM((2, BLOCK, 2 * BLOCK), F32)]
    est = (2 * 9 * ATTN_TILE * LANES * 2 + 2 * N_GROUPS * 2 * BLOCK * 2 * BLOCK * 4 + sum(tails) * LANES * 2
           + 2 * N_GROUPS * ATTN_TILE * LANES * 4 + 8 * ATTN_TILE * LANES * 4)
    operands = [a for a in qkvs for _ in range(3)]
    return pl.pallas_call(
        _attn_kernel,
        grid=(batch, SLABS, per_seq),
        in_specs=in_specs,
        out_specs=pl.BlockSpec((ATTN_TILE, LANES), lambda b, s, n: (b * per_seq + n, s)),
        out_shape=jax.ShapeDtypeStruct((t, GROUP_WIDTH), BF16),
        scratch_shapes=scratch,
        compiler_params=pltpu.CompilerParams(dimension_semantics=("arbitrary", "arbitrary", "arbitrary"),
                                             vmem_limit_bytes=_vmem_limit(est)),
        name="attn",
    )(*operands, bias)


def _mixout_kernel(x_ref, mod_ref, mix_ref, yc_ref, gate_ref, wao_ref, wco_ref, wo_ref, out_ref, *, d):
    _, _, gate = _mod_rows(mod_ref, 1)
    y_attn = jnp.dot(mix_ref[...], wao_ref[...], preferred_element_type=F32)
    y_conv = jnp.dot(yc_ref[...], wco_ref[...], preferred_element_type=F32)
    g_conv = gate_ref[:, 0:d].astype(F32)
    g_attn = gate_ref[:, d:2 * d].astype(F32)
    merged = _sigmoid(g_conv) * y_conv + _sigmoid(g_attn) * y_attn
    y = jnp.dot(merged.astype(BF16), wo_ref[...], preferred_element_type=F32)
    out_ref[...] = x_ref[...] + gate * y


def _mixout(x, mod, layer, mix, yc, gates, wao, wco, wo, seq):
    t, d = x.shape
    tm = FFN_TILE
    per_seq = seq // tm
    row = lambda w: pl.BlockSpec((tm, w), lambda i: (i, 0))
    est = (2 * d * d + GROUP_WIDTH * d) * 2 + 2 * tm * (2 * d * 4 + GROUP_WIDTH * 2 + 3 * d * 2) + tm * d * 4 * 6
    return pl.pallas_call(
        functools.partial(_mixout_kernel, d=d),
        grid=(t // tm,),
        in_specs=[row(d), pl.BlockSpec((1, 1, N_SUB * 3, d), lambda i: (layer, i // per_seq, 0, 0)),
                  row(GROUP_WIDTH), row(d), row(2 * d),
                  _resident((GROUP_WIDTH, d), (layer,)), _resident((d, d), (layer,)), _resident((d, d), (layer,))],
        out_specs=row(d),
        out_shape=jax.ShapeDtypeStruct((t, d), F32),
        compiler_params=pltpu.CompilerParams(dimension_semantics=("arbitrary",),
                                             vmem_limit_bytes=_vmem_limit(est)),
        name="mixout",
    )(x, mod, mix, yc, gates, wao, wco, wo)


def kernel(x, c, ada_w, ada_b, norm_g, ffn_w_gate, ffn_w_up, ffn_w_down, w_in, conv_w, w_conv_out, w_attn_out,
           w_o, rel_bias, final_g):
    batch, seq, d = x.shape
    depth = ada_w.shape[0]
    assert seq % ATTN_TILE == 0 and ATTN_TILE % TOKEN_TILE == 0 and seq % FFN_TILE == 0
    assert all(win // dil == BLOCK and TOKEN_TILE % (dil * 32) == 0 or dil == 1 for win, dil in DILATION_GROUPS)
    assert w_in.shape[-1] == 3 * QKV_WIDTH + 5 * d and ada_w.shape[-1] == N_SUB * 3 * d

    mod = _ada_mod(c, ada_w, ada_b)
    bias = _t5_bias(rel_bias)
    xt = x.reshape(batch * seq, d)
    fg = final_g.reshape(1, d)
    wg, wu, wd = ffn_w_gate.astype(BF16), ffn_w_up.astype(BF16), ffn_w_down.astype(BF16)
    win, wao, wco, wo = (w.astype(BF16) for w in (w_in, w_attn_out, w_conv_out, w_o))
    for l in range(depth):
        g = [norm_g[l, j].reshape(1, d) for j in range(N_SUB)]
        xt = _ffn(xt, mod, l, 0, g[0], fg, wg, wu, wd, seq)
        *qkvs, yc, gates = _inproj(xt, mod, l, g[1], win, conv_w, seq)
        mix = _attention(qkvs, bias, batch, seq)
        xt = _mixout(xt, mod, l, mix, yc, gates, wao, wco, wo, seq)
        xt = _ffn(xt, mod, l, 2, g[2], fg, wg, wu, wd, seq, final=l == depth - 1)
    return xt.reshape(batch, seq, d)
```

```python
import functools

import numpy as np
import jax
import jax.numpy as jnp
from jax import lax
from jax.experimental import pallas as pl
from jax.experimental.pallas import tpu as pltpu

F32 = jnp.float32
BF16 = jnp.bfloat16

HEAD_DIM = 64
HEADS_PER_GROUP = 8
DILATION_GROUPS = ((128, 1), (512, 4), (2048, 16))
N_GROUPS = len(DILATION_GROUPS)
GROUP_WIDTH = HEADS_PER_GROUP * HEAD_DIM
QKV_WIDTH = N_GROUPS * GROUP_WIDTH
NUM_BUCKETS = 32
MAX_DISTANCE = 2048
BLOCK = 128
CONV_K = 3
N_SUB = 3
EPS = 1e-6
NEG_INF = -1e30

V7X_VMEM_BYTES = 64 * 1024 * 1024
LANES = 128
MXU_TILE = 256
F32_SUBLANES = 8
BF16_SUBLANES = 16

FFN_TILE = 1024
FFN_CHUNKS = 2
TOKEN_TILE = 512
ADA_COL_TILE = 1152
MAX_DILATION = max(d for _, d in DILATION_GROUPS)
ATTN_TILE = BLOCK * MAX_DILATION
SLABS = GROUP_WIDTH // LANES
DEINTERLEAVE_STEP = 4


def _vmem_limit(nbytes):
    return int(min(nbytes * 5 // 4 + (4 << 20), V7X_VMEM_BYTES - (6 << 20)))


def _sigmoid(x):
    return 1.0 / (1.0 + jnp.exp(-x))


def _norm_mod(x, g, shift, scale):
    ms = jnp.mean(x * x, axis=-1, keepdims=True)
    y = x * lax.rsqrt(ms + EPS) * g
    return y * (1.0 + scale) + shift


def _mod_rows(mod_ref, sub):
    base = 3 * sub
    return (mod_ref[0, 0, base:base + 1, :], mod_ref[0, 0, base + 1:base + 2, :],
            mod_ref[0, 0, base + 2:base + 3, :])


def _resident(shape, lead=()):
    index = tuple(lead) + (0,) * len(shape)
    return pl.BlockSpec((None,) * len(lead) + tuple(shape), lambda *_: index, pipeline_mode=pl.Buffered(1))


def _cast_band(rows, steps):
    band = BF16_SUBLANES
    while rows % band or rows // band > steps:
        band += BF16_SUBLANES
    return band


def _casts(sources, steps):
    in_specs, out_specs, out_shapes, operands = [], [], [], []
    for arr, lead in sources:
        rows, cols = arr.shape[-2:]
        band = _cast_band(rows, steps)
        last = rows // band - 1
        in_specs.append(pl.BlockSpec((None,) * len(lead) + (band, cols),
                                     lambda i, lead=lead, last=last: (*lead, jnp.minimum(i, last), 0)))
        out_specs.append(pl.BlockSpec((band, cols), lambda i, last=last: (jnp.minimum(i, last), 0)))
        out_shapes.append(jax.ShapeDtypeStruct((rows, cols), BF16))
        operands.append(arr)
    return in_specs, out_specs, out_shapes, operands


def _with_casts(body, n_in, n_out, n_cast):
    def kernel(*refs):
        ins, rest = refs[:n_in], refs[n_in:]
        cast_in, rest = rest[:n_cast], rest[n_cast:]
        outs, rest = rest[:n_out], rest[n_out:]
        cast_out, scratch = rest[:n_cast], rest[n_cast:]
        for src, dst in zip(cast_in, cast_out):
            dst[...] = src[...].astype(BF16)
        body(*ins, *outs, *scratch)
    return kernel


def _ada_kernel(c_ref, w_ref, b_ref, o_ref):
    c = c_ref[...]
    cs = (c * _sigmoid(c)).astype(BF16)
    o_ref[0] = jnp.dot(cs, w_ref[0].astype(BF16), preferred_element_type=F32) + b_ref[0]


def _ada_mod(c, ada_w, ada_b):
    depth, d, n = ada_w.shape
    b = c.shape[0]
    tn = ADA_COL_TILE
    out = pl.pallas_call(
        _ada_kernel,
        grid=(depth, n // tn),
        in_specs=[pl.BlockSpec((b, d), lambda l, j: (0, 0)),
                  pl.BlockSpec((1, d, tn), lambda l, j: (l, 0, j)),
                  pl.BlockSpec((1, 1, tn), lambda l, j: (l, 0, j))],
        out_specs=pl.BlockSpec((1, b, tn), lambda l, j: (l, 0, j)),
        out_shape=jax.ShapeDtypeStruct((depth, b, n), F32),
        compiler_params=pltpu.CompilerParams(
            dimension_semantics=("arbitrary", "arbitrary"),
            vmem_limit_bytes=_vmem_limit(2 * d * tn * 4 + d * tn * 2)),
        name="ada_mod",
    )(c, ada_w, ada_b.reshape(depth, 1, n))
    return out.reshape(depth, b, N_SUB * 3, d)


def _t5_bucket(dist):
    exact = NUM_BUCKETS // 2
    d = np.maximum(dist, 1).astype(np.float32)
    large = exact + (np.log(d / exact) / np.log(MAX_DISTANCE / exact) * (NUM_BUCKETS - exact)).astype(np.int32)
    large = np.minimum(large, NUM_BUCKETS - 1)
    return np.where(dist < exact, dist, large).astype(np.int32)


def _bucket_maps():
    i = np.arange(BLOCK)[:, None]
    j = np.arange(2 * BLOCK)[None, :]
    rel = np.maximum(i - j + BLOCK, 0)
    return np.stack([_t5_bucket(rel * dil) for _, dil in DILATION_GROUPS]).astype(np.int32)


def _band_caps():
    i = np.arange(BLOCK)[:, None]
    j = np.arange(2 * BLOCK)[None, :]
    in_band = j <= i + BLOCK
    valid = np.stack([(j >= i) & in_band, (j >= BLOCK) & in_band])
    return np.where(valid, np.inf, NEG_INF).astype(np.float32)


def _bias_kernel(tab_ref, bucket_ref, o_ref):
    g = pl.program_id(0)
    bk = bucket_ref[0]
    for h in range(HEADS_PER_GROUP):
        acc = jnp.zeros(bk.shape, F32)
        for b in range(NUM_BUCKETS):
            acc = jnp.where(bk == b, tab_ref[b, g * HEADS_PER_GROUP + h], acc)
        o_ref[0, h] = acc


def _t5_bias(rel_bias):
    buckets = jnp.asarray(_bucket_maps())
    return pl.pallas_call(
        _bias_kernel,
        grid=(N_GROUPS,),
        in_specs=[pl.BlockSpec(memory_space=pltpu.SMEM),
                  pl.BlockSpec((1, BLOCK, 2 * BLOCK), lambda g: (g, 0, 0))],
        out_specs=pl.BlockSpec((1, HEADS_PER_GROUP, BLOCK, 2 * BLOCK), lambda g: (g, 0, 0, 0)),
        out_shape=jax.ShapeDtypeStruct((N_GROUPS, HEADS_PER_GROUP, BLOCK, 2 * BLOCK), F32),
        compiler_params=pltpu.CompilerParams(dimension_semantics=("arbitrary",)),
        name="t5_bias",
    )(rel_bias, buckets)


def _ff_chunks(dff):
    tiles = dff // MXU_TILE
    assert tiles * MXU_TILE == dff
    sizes = [tiles // FFN_CHUNKS + (1 if c < tiles % FFN_CHUNKS else 0) for c in range(FFN_CHUNKS)]
    edges = [0]
    for sz in sizes:
        edges.append(edges[-1] + sz * MXU_TILE)
    return list(zip(edges[:-1], edges[1:]))


def _ffn_kernel(x_ref, mod_ref, g_ref, fg_ref, wg_ref, wu_ref, wd_ref, o_ref, *, sub, final):
    x = x_ref[...]
    shift, scale, gate = _mod_rows(mod_ref, sub)
    h = _norm_mod(x, g_ref[...], shift, scale).astype(BF16)
    y = None
    for lo, hi in _ff_chunks(wg_ref.shape[1]):
        a = jnp.dot(h, wg_ref[:, lo:hi], preferred_element_type=F32)
        u = jnp.dot(h, wu_ref[:, lo:hi], preferred_element_type=F32)
        act = (a * _sigmoid(a) * u).astype(BF16)
        part = jnp.dot(act, wd_ref[lo:hi, :], preferred_element_type=F32)
        y = part if y is None else y + part
    out = x + (0.5 * gate) * y
    if final:
        ms = jnp.mean(out * out, axis=-1, keepdims=True)
        out = out * lax.rsqrt(ms + EPS) * fg_ref[...]
    o_ref[...] = out


def _ffn(x, mod, layer, sub, g, final_g, wg, wu, wd, seq, final=False, cast=()):
    t, d = x.shape
    dff = wg.shape[-1]
    tm = FFN_TILE
    per_seq = seq // tm
    steps = t // tm
    c_in, c_out, c_shapes, c_ops = _casts(cast, steps)
    est = 3 * d * dff * 2 + 4 * tm * d * 4 + tm * (dff // FFN_CHUNKS + MXU_TILE) * 10 + tm * d * 10 + (6 << 20)
    body = functools.partial(_ffn_kernel, sub=sub, final=final)
    out, *cast_out = pl.pallas_call(
        _with_casts(body, 7, 1, len(cast)),
        grid=(steps,),
        in_specs=[pl.BlockSpec((tm, d), lambda i: (i, 0)),
                  pl.BlockSpec((1, 1, N_SUB * 3, d), lambda i: (layer, i // per_seq, 0, 0)),
                  _resident((1, d)), _resident((1, d)),
                  _resident((d, dff)), _resident((d, dff)), _resident((dff, d))] + c_in,
        out_specs=[pl.BlockSpec((tm, d), lambda i: (i, 0))] + c_out,
        out_shape=[jax.ShapeDtypeStruct((t, d), F32)] + c_shapes,
        compiler_params=pltpu.CompilerParams(dimension_semantics=("arbitrary",),
                                             vmem_limit_bytes=_vmem_limit(est)),
        name="ffn_final" if final else "ffn",
    )(x, mod, g, final_g, wg, wu, wd, *c_ops)
    return out, cast_out


def _residue(grun, groups, r):
    low, weight, g = 0, 1, groups
    while g > 1:
        g //= DEINTERLEAVE_STEP
        low += ((grun // g) % DEINTERLEAVE_STEP) * weight
        weight *= DEINTERLEAVE_STEP
    return low + r * weight


def _inproj_kernel(x_ref, mod_ref, g_ref, w_ref, cw_ref, qkv1_ref, qkv2_ref, qkv3_ref, yc_ref,
                   gate_ref, pbuf, ubuf, tbuf, *, per_seq, d):
    tm = x_ref.shape[0]

    @pl.when(pl.program_id(0) % per_seq == 0)
    def _():
        pbuf[0:F32_SUBLANES, :] = jnp.zeros((F32_SUBLANES, d), F32)

    shift, scale, _ = _mod_rows(mod_ref, 1)
    h = _norm_mod(x_ref[...], g_ref[...], shift, scale).astype(BF16)

    def proj(lo, width):
        return jnp.dot(h, w_ref[:, lo:lo + width], preferred_element_type=F32)

    base = 3 * QKV_WIDTH
    p = proj(base + d, d) * proj(base + 2 * d, d)
    pbuf[F32_SUBLANES:F32_SUBLANES + tm, :] = p
    conv = (cw_ref[0:1, :] * pbuf[F32_SUBLANES - 2:F32_SUBLANES - 2 + tm, :]
            + cw_ref[1:2, :] * pbuf[F32_SUBLANES - 1:F32_SUBLANES - 1 + tm, :]
            + cw_ref[2:3, :] * p)
    yc_ref[...] = (proj(base, d) * conv).astype(BF16)
    pbuf[0:F32_SUBLANES, :] = pbuf[tm:tm + F32_SUBLANES, :]

    outs = (qkv1_ref, qkv2_ref, qkv3_ref)
    for part in range(3):
        u = proj(part * QKV_WIDTH, QKV_WIDTH)
        if part == 0:
            u = u * (HEAD_DIM ** -0.5)
        col = part * GROUP_WIDTH
        for grp, (_, dil) in enumerate(DILATION_GROUPS):
            ug = u[:, grp * GROUP_WIDTH:(grp + 1) * GROUP_WIDTH]
            if dil == 1:
                outs[grp][:, col:col + GROUP_WIDTH] = ug.astype(BF16)
                continue
            buf = ubuf.at[grp - 1]
            for s in range(SLABS):
                buf[s] = ug[:, s * LANES:(s + 1) * LANES]
            step, groups = 1, 1
            while step * DEINTERLEAVE_STEP < dil:
                assert buf is not tbuf, "one intermediate buffer: at most two passes"
                run = tm // (groups * DEINTERLEAVE_STEP)
                for s in range(SLABS):
                    for grun in range(groups):
                        for r in range(DEINTERLEAVE_STEP):
                            dst = (grun * DEINTERLEAVE_STEP + r) * run
                            tbuf[s, dst:dst + run, :] = (
                                buf[s, pl.ds(grun * (tm // groups) + r, run, stride=DEINTERLEAVE_STEP), :])
                buf, step, groups = tbuf, step * DEINTERLEAVE_STEP, groups * DEINTERLEAVE_STEP
            last = dil // step
            run = tm // dil
            for s in range(SLABS):
                for grun in range(groups):
                    for r in range(last):
                        res = _residue(grun, groups, r)
                        outs[grp][res * run:(res + 1) * run, col + s * LANES:col + (s + 1) * LANES] = (
                            buf[s, pl.ds(grun * (tm // groups) + r, run, stride=last), :].astype(BF16))

    for c in range(2):
        gate_ref[:, c * d:(c + 1) * d] = proj(base + 3 * d + c * d, d).astype(BF16)


def _inproj(x, mod, layer, g, w_in, conv_w, seq, cast=()):
    t, d = x.shape
    n = w_in.shape[-1]
    tm = TOKEN_TILE
    per_seq = seq // tm
    est = (d * n * 2 + 2 * tm * d * 4 + 2 * tm * (3 * QKV_WIDTH + 3 * d) * 2 + (tm + 8) * d * 4
           + N_GROUPS * SLABS * tm * LANES * 4 + tm * d * 4 * 8)
    row = lambda w: pl.BlockSpec((tm, w), lambda i: (i, 0))
    c_in, c_out, c_shapes, c_ops = _casts(cast, t // tm)
    body = functools.partial(_inproj_kernel, per_seq=per_seq, d=d)
    *outs, = pl.pallas_call(
        _with_casts(body, 5, 5, len(cast)),
        grid=(t // tm,),
        in_specs=[row(d),
                  pl.BlockSpec((1, 1, N_SUB * 3, d), lambda i: (layer, i // per_seq, 0, 0)),
                  _resident((1, d)), _resident((d, n)), _resident((CONV_K, d), (layer,))] + c_in,
        out_specs=[row(QKV_WIDTH), row(QKV_WIDTH), row(QKV_WIDTH), row(d), row(2 * d)] + c_out,
        out_shape=[jax.ShapeDtypeStruct((t, QKV_WIDTH), BF16)] * 3
                  + [jax.ShapeDtypeStruct((t, d), BF16), jax.ShapeDtypeStruct((t, 2 * d), BF16)] + c_shapes,
        scratch_shapes=[pltpu.VMEM((tm + F32_SUBLANES, d), F32),
                        pltpu.VMEM((N_GROUPS - 1, SLABS, tm, LANES), F32),
                        pltpu.VMEM((SLABS, tm, LANES), F32)],
        compiler_params=pltpu.CompilerParams(dimension_semantics=("arbitrary",),
                                             vmem_limit_bytes=_vmem_limit(est)),
        name="inproj",
    )(x, mod, g, w_in, conv_w, *c_ops)
    return outs[:5], outs[5:]


def _attn_kernel(q1, k1, v1, q2, k2, v2, q3, k3, v3, bias_ref, cap_ref, out_ref,
                 pk1, pv1, pk2, pv2, pk3, pv3, o2, l2, o3, l3):
    n = pl.program_id(2)
    lane = lax.broadcasted_iota(jnp.int32, (1, LANES), 1)
    low = lane < HEAD_DIM
    keeps = (low.astype(BF16), 1 - low.astype(BF16))
    nt = (((1,), (1,)), ((), ()))

    def unit(grp, q, k, v, cap):
        v_ext = jnp.concatenate([v, jnp.ones((2 * BLOCK, LANES), BF16)], axis=1)
        parts = []
        for half in range(2):
            s = lax.dot_general(q * keeps[half], k, nt, preferred_element_type=F32)
            s = jnp.minimum(s + bias_ref[grp, half], cap)
            m = jnp.max(s, axis=-1, keepdims=True)
            p = jnp.exp(s - m).astype(BF16)
            oe = jnp.dot(p, v_ext, preferred_element_type=F32)
            parts.append((oe[:, :LANES], oe[:, LANES:], m))
        (oa, la, ma), (ob, lb, mb) = parts
        l = jnp.where(low, la, lb)
        return jnp.where(low, oa, ob) / l, jnp.where(low, ma, mb) + jnp.log(l)

    def cat(a, b):
        return jnp.concatenate([a, b], axis=0)

    first = n == 0
    tail_cap = cap_ref[first.astype(jnp.int32)]
    full_cap = cap_ref[0]

    @pl.when(first)
    def _():
        for ref in (pk1, pv1, pk2, pv2, pk3, pv3):
            ref[...] = jnp.zeros(ref.shape, BF16)

    d3 = DILATION_GROUPS[2][1]
    run3 = TOKEN_TILE // d3

    def gather3(ref, r):
        return jnp.concatenate([ref[j * TOKEN_TILE + r * run3:j * TOKEN_TILE + (r + 1) * run3, :]
                                for j in range(ATTN_TILE // TOKEN_TILE)], axis=0)

    for r in range(d3):
        o, lse = unit(2, gather3(q3, r), cat(gather3(pk3, r), gather3(k3, r)),
                      cat(gather3(pv3, r), gather3(v3, r)), tail_cap)
        o3[pl.ds(r, BLOCK, stride=d3), :] = o
        l3[pl.ds(r, BLOCK, stride=d3), :] = lse

    d2 = DILATION_GROUPS[1][1]
    for n2 in range(ATTN_TILE // TOKEN_TILE):
        for r in range(d2):
            cur = slice(n2 * TOKEN_TILE + r * BLOCK, n2 * TOKEN_TILE + (r + 1) * BLOCK)
            if n2 == 0:
                run = slice(r * BLOCK, (r + 1) * BLOCK)
                k, v, cap = cat(pk2[run, :], k2[cur, :]), cat(pv2[run, :], v2[cur, :]), tail_cap
            else:
                prev = slice(cur.start - TOKEN_TILE, cur.stop - TOKEN_TILE)
                k, v, cap = cat(k2[prev, :], k2[cur, :]), cat(v2[prev, :], v2[cur, :]), full_cap
            o, lse = unit(1, q2[cur, :], k, v, cap)
            o2[pl.ds(n2 * TOKEN_TILE + r, BLOCK, stride=d2), :] = o
            l2[pl.ds(n2 * TOKEN_TILE + r, BLOCK, stride=d2), :] = lse

    for u in range(ATTN_TILE // BLOCK):
        cur = slice(u * BLOCK, (u + 1) * BLOCK)
        if u == 0:
            k, v, cap = cat(pk1[...], k1[cur, :]), cat(pv1[...], v1[cur, :]), tail_cap
        else:
            both = slice((u - 1) * BLOCK, (u + 1) * BLOCK)
            k, v, cap = k1[both, :], v1[both, :], full_cap
        oa, la = unit(0, q1[cur, :], k, v, cap)
        lb, lc = l2[cur, :], l3[cur, :]
        m = jnp.maximum(jnp.maximum(la, lb), lc)
        ea, eb, ec = jnp.exp(la - m), jnp.exp(lb - m), jnp.exp(lc - m)
        mix = (ea * oa + eb * o2[cur, :] + ec * o3[cur, :]) / (ea + eb + ec)
        out_ref[cur, :] = mix.astype(BF16)

    @pl.when(first)
    def _():
        pk1[...] = k1[ATTN_TILE - BLOCK:, :]
        pv1[...] = v1[ATTN_TILE - BLOCK:, :]
        pk2[...] = k2[ATTN_TILE - TOKEN_TILE:, :]
        pv2[...] = v2[ATTN_TILE - TOKEN_TILE:, :]
        pk3[...] = k3[...]
        pv3[...] = v3[...]


def _attention(qkvs, bias, batch, seq):
    t = qkvs[0].shape[0]
    assert seq == 2 * ATTN_TILE, "the previous-tile carry assumes two attention tiles per sequence"
    per_seq = seq // ATTN_TILE

    def part(p):
        return pl.BlockSpec((ATTN_TILE, LANES), lambda b, s, n: (b * per_seq + n, p * SLABS + s))

    in_specs = [part(p) for _ in range(N_GROUPS) for p in range(3)]
    in_specs.append(pl.BlockSpec((N_GROUPS, 2, BLOCK, 2 * BLOCK), lambda b, s, n: (0, s, 0, 0)))
    in_specs.append(_resident((2, BLOCK, 2 * BLOCK)))
    tails = [BLOCK, BLOCK, TOKEN_TILE, TOKEN_TILE, ATTN_TILE, ATTN_TILE]
    scratch = [pltpu.VMEM((rows, LANES), BF16) for rows in tails]
    scratch += [pltpu.VMEM((ATTN_TILE, LANES), F32)] * (2 * (N_GROUPS - 1))
    est = (2 * 9 * ATTN_TILE * LANES * 2 + 2 * N_GROUPS * 2 * BLOCK * 2 * BLOCK * 4 + sum(tails) * LANES * 2
           + 2 * N_GROUPS * ATTN_TILE * LANES * 4 + 8 * ATTN_TILE * LANES * 4)
    operands = [a for a in qkvs for _ in range(3)]
    return pl.pallas_call(
        _attn_kernel,
        grid=(batch, SLABS, per_seq),
        in_specs=in_specs,
        out_specs=pl.BlockSpec((ATTN_TILE, LANES), lambda b, s, n: (b * per_seq + n, s)),
        out_shape=jax.ShapeDtypeStruct((t, GROUP_WIDTH), BF16),
        scratch_shapes=scratch,
        compiler_params=pltpu.CompilerParams(dimension_semantics=("arbitrary", "arbitrary", "arbitrary"),
                                             vmem_limit_bytes=_vmem_limit(est)),
        name="attn",
    )(*operands, bias, jnp.asarray(_band_caps()))


def _mixout_kernel(x_ref, mod_ref, mix_ref, yc_ref, gate_ref, wao_ref, wco_ref, wo_ref, out_ref, *, d):
    _, _, gate = _mod_rows(mod_ref, 1)
    y_attn = jnp.dot(mix_ref[...], wao_ref[...], preferred_element_type=F32)
    y_conv = jnp.dot(yc_ref[...], wco_ref[...], preferred_element_type=F32)
    g_conv = gate_ref[:, 0:d].astype(F32)
    g_attn = gate_ref[:, d:2 * d].astype(F32)
    merged = _sigmoid(g_conv) * y_conv + _sigmoid(g_attn) * y_attn
    y = jnp.dot(merged.astype(BF16), wo_ref[...], preferred_element_type=F32)
    out_ref[...] = x_ref[...] + gate * y


def _mixout(x, mod, layer, mix, yc, gates, wao, wco, wo, seq, cast=()):
    t, d = x.shape
    tm = FFN_TILE
    per_seq = seq // tm
    row = lambda w: pl.BlockSpec((tm, w), lambda i: (i, 0))
    est = (2 * d * d + GROUP_WIDTH * d) * 2 + 2 * tm * (2 * d * 4 + GROUP_WIDTH * 2 + 3 * d * 2) + tm * d * 4 * 6
    c_in, c_out, c_shapes, c_ops = _casts(cast, t // tm)
    out, *cast_out = pl.pallas_call(
        _with_casts(functools.partial(_mixout_kernel, d=d), 8, 1, len(cast)),
        grid=(t // tm,),
        in_specs=[row(d), pl.BlockSpec((1, 1, N_SUB * 3, d), lambda i: (layer, i // per_seq, 0, 0)),
                  row(GROUP_WIDTH), row(d), row(2 * d),
                  _resident((GROUP_WIDTH, d)), _resident((d, d)), _resident((d, d))] + c_in,
        out_specs=[row(d)] + c_out,
        out_shape=[jax.ShapeDtypeStruct((t, d), F32)] + c_shapes,
        compiler_params=pltpu.CompilerParams(dimension_semantics=("arbitrary",),
                                             vmem_limit_bytes=_vmem_limit(est)),
        name="mixout",
    )(x, mod, mix, yc, gates, wao, wco, wo, *c_ops)
    return out, cast_out


def kernel(x, c, ada_w, ada_b, norm_g, ffn_w_gate, ffn_w_up, ffn_w_down, w_in, conv_w, w_conv_out, w_attn_out,
           w_o, rel_bias, final_g):
    batch, seq, d = x.shape
    depth = ada_w.shape[0]
    assert seq % ATTN_TILE == 0 and ATTN_TILE % TOKEN_TILE == 0 and seq % FFN_TILE == 0
    assert all(win // dil == BLOCK and TOKEN_TILE % (dil * 32) == 0 or dil == 1 for win, dil in DILATION_GROUPS)
    assert w_in.shape[-1] == 3 * QKV_WIDTH + 5 * d and ada_w.shape[-1] == N_SUB * 3 * d

    mod = _ada_mod(c, ada_w, ada_b)
    bias = _t5_bias(rel_bias)
    xt = x.reshape(batch * seq, d)
    fg = final_g.reshape(1, d)
    ffn_w = [w[0, 0].astype(BF16) for w in (ffn_w_gate, ffn_w_up, ffn_w_down)]
    for l in range(depth):
        g = [norm_g[l, j].reshape(1, d) for j in range(N_SUB)]
        xt, (win,) = _ffn(xt, mod, l, 0, g[0], fg, *ffn_w, seq, cast=[(w_in, (l,))])
        (*qkvs, yc, gates), (wao, wco, wo) = _inproj(
            xt, mod, l, g[1], win, conv_w, seq, cast=[(w_attn_out, (l,)), (w_conv_out, (l,)), (w_o, (l,))])
        mix = _attention(qkvs, bias, batch, seq)
        xt, ffn_w = _mixout(xt, mod, l, mix, yc, gates, wao, wco, wo, seq,
                            cast=[(w, (l, 1)) for w in (ffn_w_gate, ffn_w_up, ffn_w_down)])
        nxt = [(w, (l + 1, 0)) for w in (ffn_w_gate, ffn_w_up, ffn_w_down)] if l + 1 < depth else []
        xt, ffn_w = _ffn(xt, mod, l, 2, g[2], fg, *ffn_w, seq, final=l == depth - 1, cast=nxt)
    return xt.reshape(batch, seq, d)
```

```python
import functools

import numpy as np
import jax
import jax.numpy as jnp
from jax import lax
from jax.experimental import pallas as pl
from jax.experimental.pallas import tpu as pltpu

F32 = jnp.float32
BF16 = jnp.bfloat16

HEAD_DIM = 64
HEADS_PER_GROUP = 8
DILATION_GROUPS = ((128, 1), (512, 4), (2048, 16))
N_GROUPS = len(DILATION_GROUPS)
GROUP_WIDTH = HEADS_PER_GROUP * HEAD_DIM
QKV_WIDTH = N_GROUPS * GROUP_WIDTH
NUM_BUCKETS = 32
MAX_DISTANCE = 2048
BLOCK = 128
CONV_K = 3
N_SUB = 3
EPS = 1e-6
LOG2E = float(np.log2(np.e))
LN2 = float(np.log(2.0))

V7X_VMEM_BYTES = 64 * 1024 * 1024
LANES = 128
MXU_TILE = 256
F32_SUBLANES = 8
BF16_SUBLANES = 16

FFN_TILE = 1024
FFN_CHUNKS = 2
TOKEN_TILE = 512
ADA_COL_TILE = 1152
MAX_DILATION = max(d for _, d in DILATION_GROUPS)
ATTN_TILE = BLOCK * MAX_DILATION
SLABS = GROUP_WIDTH // LANES
DEINTERLEAVE_STEP = 4


def _vmem_limit(nbytes):
    return int(min(nbytes * 5 // 4 + (4 << 20), V7X_VMEM_BYTES - (6 << 20)))


def _sigmoid(x):
    return 1.0 / (1.0 + jnp.exp(-x))


def _norm_mod(x, g, shift, scale):
    ms = jnp.mean(x * x, axis=-1, keepdims=True)
    y = x * lax.rsqrt(ms + EPS) * g
    return y * (1.0 + scale) + shift


def _mod_rows(mod_ref, sub):
    base = 3 * sub
    return (mod_ref[0, 0, base:base + 1, :], mod_ref[0, 0, base + 1:base + 2, :],
            mod_ref[0, 0, base + 2:base + 3, :])


def _resident(shape, lead=()):
    index = tuple(lead) + (0,) * len(shape)
    return pl.BlockSpec((None,) * len(lead) + tuple(shape), lambda *_: index, pipeline_mode=pl.Buffered(1))


def _cast_band(rows, steps):
    band = BF16_SUBLANES
    while rows % band or rows // band > steps:
        band += BF16_SUBLANES
    return band


def _casts(sources, steps):
    in_specs, out_specs, out_shapes, operands = [], [], [], []
    for arr, lead in sources:
        rows, cols = arr.shape[-2:]
        band = _cast_band(rows, steps)
        last = rows // band - 1
        in_specs.append(pl.BlockSpec((None,) * len(lead) + (band, cols),
                                     lambda i, lead=lead, last=last: (*lead, jnp.minimum(i, last), 0)))
        out_specs.append(pl.BlockSpec((band, cols), lambda i, last=last: (jnp.minimum(i, last), 0)))
        out_shapes.append(jax.ShapeDtypeStruct((rows, cols), BF16))
        operands.append(arr)
    return in_specs, out_specs, out_shapes, operands


def _with_casts(body, n_in, n_out, n_cast):
    def kernel(*refs):
        ins, rest = refs[:n_in], refs[n_in:]
        cast_in, rest = rest[:n_cast], rest[n_cast:]
        outs, rest = rest[:n_out], rest[n_out:]
        cast_out, scratch = rest[:n_cast], rest[n_cast:]
        for src, dst in zip(cast_in, cast_out):
            dst[...] = src[...].astype(BF16)
        body(*ins, *outs, *scratch)
    return kernel


def _ada_kernel(c_ref, w_ref, b_ref, o_ref):
    c = c_ref[...]
    cs = (c * _sigmoid(c)).astype(BF16)
    o_ref[0] = jnp.dot(cs, w_ref[0].astype(BF16), preferred_element_type=F32) + b_ref[0]


def _ada_mod(c, ada_w, ada_b):
    depth, d, n = ada_w.shape
    b = c.shape[0]
    tn = ADA_COL_TILE
    out = pl.pallas_call(
        _ada_kernel,
        grid=(depth, n // tn),
        in_specs=[pl.BlockSpec((b, d), lambda l, j: (0, 0)),
                  pl.BlockSpec((1, d, tn), lambda l, j: (l, 0, j)),
                  pl.BlockSpec((1, 1, tn), lambda l, j: (l, 0, j))],
        out_specs=pl.BlockSpec((1, b, tn), lambda l, j: (l, 0, j)),
        out_shape=jax.ShapeDtypeStruct((depth, b, n), F32),
        compiler_params=pltpu.CompilerParams(
            dimension_semantics=("arbitrary", "arbitrary"),
            vmem_limit_bytes=_vmem_limit(2 * d * tn * 4 + d * tn * 2)),
        name="ada_mod",
    )(c, ada_w, ada_b.reshape(depth, 1, n))
    return out.reshape(depth, b, N_SUB * 3, d)


def _t5_bucket(dist):
    exact = NUM_BUCKETS // 2
    d = np.maximum(dist, 1).astype(np.float32)
    large = exact + (np.log(d / exact) / np.log(MAX_DISTANCE / exact) * (NUM_BUCKETS - exact)).astype(np.int32)
    large = np.minimum(large, NUM_BUCKETS - 1)
    return np.where(dist < exact, dist, large).astype(np.int32)


def _bucket_maps():
    i = np.arange(BLOCK)[:, None]
    j = np.arange(2 * BLOCK)[None, :]
    rel = np.maximum(i - j + BLOCK, 0)
    return np.stack([_t5_bucket(rel * dil) for _, dil in DILATION_GROUPS]).astype(np.int32)


def _band_valid():
    i = np.arange(BLOCK)[:, None]
    j = np.arange(2 * BLOCK)[None, :]
    in_band = j <= i + BLOCK
    return np.stack([(j >= i) & in_band, (j >= BLOCK) & in_band]).astype(np.int32)


def _bias_kernel(tab_ref, bucket_ref, valid_ref, o_ref):
    g = pl.program_id(0)
    bk = bucket_ref[0]
    for h in range(HEADS_PER_GROUP):
        acc = jnp.zeros(bk.shape, F32)
        for b in range(NUM_BUCKETS):
            acc = jnp.where(bk == b, tab_ref[b, g * HEADS_PER_GROUP + h] * LOG2E, acc)
        for kind in range(2):
            o_ref[0, h, kind] = jnp.where(valid_ref[kind] != 0, acc, -jnp.inf)


def _t5_bias(rel_bias):
    buckets = jnp.asarray(_bucket_maps())
    shape = (N_GROUPS, HEADS_PER_GROUP, 2, BLOCK, 2 * BLOCK)
    return pl.pallas_call(
        _bias_kernel,
        grid=(N_GROUPS,),
        in_specs=[pl.BlockSpec(memory_space=pltpu.SMEM),
                  pl.BlockSpec((1, BLOCK, 2 * BLOCK), lambda g: (g, 0, 0)),
                  pl.BlockSpec((2, BLOCK, 2 * BLOCK), lambda g: (0, 0, 0))],
        out_specs=pl.BlockSpec((1,) + shape[1:], lambda g: (g, 0, 0, 0, 0)),
        out_shape=jax.ShapeDtypeStruct(shape, F32),
        compiler_params=pltpu.CompilerParams(dimension_semantics=("arbitrary",)),
        name="t5_bias",
    )(rel_bias, buckets, jnp.asarray(_band_valid()))


def _ff_chunks(dff):
    tiles = dff // MXU_TILE
    assert tiles * MXU_TILE == dff
    sizes = [tiles // FFN_CHUNKS + (1 if c < tiles % FFN_CHUNKS else 0) for c in range(FFN_CHUNKS)]
    edges = [0]
    for sz in sizes:
        edges.append(edges[-1] + sz * MXU_TILE)
    return list(zip(edges[:-1], edges[1:]))


def _ffn_kernel(x_ref, mod_ref, g_ref, fg_ref, wg_ref, wu_ref, wd_ref, o_ref, *, sub, final):
    x = x_ref[...]
    shift, scale, gate = _mod_rows(mod_ref, sub)
    h = _norm_mod(x, g_ref[...], shift, scale).astype(BF16)
    y = None
    for lo, hi in _ff_chunks(wg_ref.shape[1]):
        a = jnp.dot(h, wg_ref[:, lo:hi], preferred_element_type=F32)
        u = jnp.dot(h, wu_ref[:, lo:hi], preferred_element_type=F32)
        act = (a * _sigmoid(a) * u).astype(BF16)
        part = jnp.dot(act, wd_ref[lo:hi, :], preferred_element_type=F32)
        y = part if y is None else y + part
    out = x + (0.5 * gate) * y
    if final:
        ms = jnp.mean(out * out, axis=-1, keepdims=True)
        out = out * lax.rsqrt(ms + EPS) * fg_ref[...]
    o_ref[...] = out


def _ffn(x, mod, layer, sub, g, final_g, wg, wu, wd, seq, final=False, cast=()):
    t, d = x.shape
    dff = wg.shape[-1]
    tm = FFN_TILE
    per_seq = seq // tm
    steps = t // tm
    c_in, c_out, c_shapes, c_ops = _casts(cast, steps)
    est = 3 * d * dff * 2 + 4 * tm * d * 4 + tm * (dff // FFN_CHUNKS + MXU_TILE) * 10 + tm * d * 10 + (6 << 20)
    body = functools.partial(_ffn_kernel, sub=sub, final=final)
    out, *cast_out = pl.pallas_call(
        _with_casts(body, 7, 1, len(cast)),
        grid=(steps,),
        in_specs=[pl.BlockSpec((tm, d), lambda i: (i, 0)),
                  pl.BlockSpec((1, 1, N_SUB * 3, d), lambda i: (layer, i // per_seq, 0, 0)),
                  _resident((1, d)), _resident((1, d)),
                  _resident((d, dff)), _resident((d, dff)), _resident((dff, d))] + c_in,
        out_specs=[pl.BlockSpec((tm, d), lambda i: (i, 0))] + c_out,
        out_shape=[jax.ShapeDtypeStruct((t, d), F32)] + c_shapes,
        compiler_params=pltpu.CompilerParams(dimension_semantics=("arbitrary",),
                                             vmem_limit_bytes=_vmem_limit(est)),
        name="ffn_final" if final else "ffn",
    )(x, mod, g, final_g, wg, wu, wd, *c_ops)
    return out, cast_out


def _residue(grun, groups, r):
    low, weight, g = 0, 1, groups
    while g > 1:
        g //= DEINTERLEAVE_STEP
        low += ((grun // g) % DEINTERLEAVE_STEP) * weight
        weight *= DEINTERLEAVE_STEP
    return low + r * weight


def _inproj_kernel(x_ref, mod_ref, g_ref, w_ref, cw_ref, qkv1_ref, qkv2_ref, qkv3_ref, yc_ref,
                   gate_ref, pbuf, ubuf, tbuf, *, per_seq, d):
    tm = x_ref.shape[0]

    @pl.when(pl.program_id(0) % per_seq == 0)
    def _():
        pbuf[0:F32_SUBLANES, :] = jnp.zeros((F32_SUBLANES, d), F32)

    shift, scale, _ = _mod_rows(mod_ref, 1)
    h = _norm_mod(x_ref[...], g_ref[...], shift, scale).astype(BF16)

    def proj(lo, width):
        return jnp.dot(h, w_ref[:, lo:lo + width], preferred_element_type=F32)

    base = 3 * QKV_WIDTH
    p = proj(base + d, d) * proj(base + 2 * d, d)
    pbuf[F32_SUBLANES:F32_SUBLANES + tm, :] = p
    conv = (cw_ref[0:1, :] * pbuf[F32_SUBLANES - 2:F32_SUBLANES - 2 + tm, :]
            + cw_ref[1:2, :] * pbuf[F32_SUBLANES - 1:F32_SUBLANES - 1 + tm, :]
            + cw_ref[2:3, :] * p)
    yc_ref[...] = (proj(base, d) * conv).astype(BF16)
    pbuf[0:F32_SUBLANES, :] = pbuf[tm:tm + F32_SUBLANES, :]

    outs = (qkv1_ref, qkv2_ref, qkv3_ref)
    for part in range(3):
        u = proj(part * QKV_WIDTH, QKV_WIDTH)
        if part == 0:
            u = u * (HEAD_DIM ** -0.5 * LOG2E)
        col = part * GROUP_WIDTH
        for grp, (_, dil) in enumerate(DILATION_GROUPS):
            ug = u[:, grp * GROUP_WIDTH:(grp + 1) * GROUP_WIDTH]
            if dil == 1:
                outs[grp][:, col:col + GROUP_WIDTH] = ug.astype(BF16)
                continue
            buf = ubuf.at[grp - 1]
            for s in range(SLABS):
                buf[s] = ug[:, s * LANES:(s + 1) * LANES]
            step, groups = 1, 1
            while step * DEINTERLEAVE_STEP < dil:
                assert buf is not tbuf, "one intermediate buffer: at most two passes"
                run = tm // (groups * DEINTERLEAVE_STEP)
                for s in range(SLABS):
                    for grun in range(groups):
                        for r in range(DEINTERLEAVE_STEP):
                            dst = (grun * DEINTERLEAVE_STEP + r) * run
                            tbuf[s, dst:dst + run, :] = (
                                buf[s, pl.ds(grun * (tm // groups) + r, run, stride=DEINTERLEAVE_STEP), :])
                buf, step, groups = tbuf, step * DEINTERLEAVE_STEP, groups * DEINTERLEAVE_STEP
            last = dil // step
            run = tm // dil
            for s in range(SLABS):
                for grun in range(groups):
                    for r in range(last):
                        res = _residue(grun, groups, r)
                        outs[grp][res * run:(res + 1) * run, col + s * LANES:col + (s + 1) * LANES] = (
                            buf[s, pl.ds(grun * (tm // groups) + r, run, stride=last), :].astype(BF16))

    for c in range(2):
        gate_ref[:, c * d:(c + 1) * d] = proj(base + 3 * d + c * d, d).astype(BF16)


def _inproj(x, mod, layer, g, w_in, conv_w, seq, cast=()):
    t, d = x.shape
    n = w_in.shape[-1]
    tm = TOKEN_TILE
    per_seq = seq // tm
    est = (d * n * 2 + 2 * tm * d * 4 + 2 * tm * (3 * QKV_WIDTH + 3 * d) * 2 + (tm + 8) * d * 4
           + N_GROUPS * SLABS * tm * LANES * 4 + tm * d * 4 * 8)
    row = lambda w: pl.BlockSpec((tm, w), lambda i: (i, 0))
    c_in, c_out, c_shapes, c_ops = _casts(cast, t // tm)
    body = functools.partial(_inproj_kernel, per_seq=per_seq, d=d)
    *outs, = pl.pallas_call(
        _with_casts(body, 5, 5, len(cast)),
        grid=(t // tm,),
        in_specs=[row(d),
                  pl.BlockSpec((1, 1, N_SUB * 3, d), lambda i: (layer, i // per_seq, 0, 0)),
                  _resident((1, d)), _resident((d, n)), _resident((CONV_K, d), (layer,))] + c_in,
        out_specs=[row(QKV_WIDTH), row(QKV_WIDTH), row(QKV_WIDTH), row(d), row(2 * d)] + c_out,
        out_shape=[jax.ShapeDtypeStruct((t, QKV_WIDTH), BF16)] * 3
                  + [jax.ShapeDtypeStruct((t, d), BF16), jax.ShapeDtypeStruct((t, 2 * d), BF16)] + c_shapes,
        scratch_shapes=[pltpu.VMEM((tm + F32_SUBLANES, d), F32),
                        pltpu.VMEM((N_GROUPS - 1, SLABS, tm, LANES), F32),
                        pltpu.VMEM((SLABS, tm, LANES), F32)],
        compiler_params=pltpu.CompilerParams(dimension_semantics=("arbitrary",),
                                             vmem_limit_bytes=_vmem_limit(est)),
        name="inproj",
    )(x, mod, g, w_in, conv_w, *c_ops)
    return outs[:5], outs[5:]


def _attn_kernel(q1, k1, v1, q2, k2, v2, q3, k3, v3, bias_ref, out_ref,
                 pk1, pv1, pk2, pv2, pk3, pv3, o2, l2, o3, l3):
    n = pl.program_id(2)
    lane = lax.broadcasted_iota(jnp.int32, (1, LANES), 1)
    low = lane < HEAD_DIM
    keeps = (low.astype(BF16), 1 - low.astype(BF16))
    nt = (((1,), (1,)), ((), ()))

    def unit(grp, q, k, v, kind):
        v_ext = jnp.concatenate([v, jnp.ones((2 * BLOCK, LANES), BF16)], axis=1)
        parts = []
        for half in range(2):
            s = lax.dot_general(q * keeps[half], k, nt, preferred_element_type=F32)
            s = s + bias_ref[grp, half, kind]
            m = jnp.max(s, axis=-1, keepdims=True)
            p = jnp.exp2(s - m).astype(BF16)
            oe = jnp.dot(p, v_ext, preferred_element_type=F32)
            parts.append((oe[:, :LANES], oe[:, LANES:], m))
        (oa, la, ma), (ob, lb, mb) = parts
        l = jnp.where(low, la, lb)
        return jnp.where(low, oa, ob) / l, jnp.where(low, ma, mb) * LN2 + jnp.log(l)

    def cat(a, b):
        return jnp.concatenate([a, b], axis=0)

    first = n == 0
    tail_kind = first.astype(jnp.int32)
    full_kind = 0

    @pl.when(first)
    def _():
        for ref in (pk1, pv1, pk2, pv2, pk3, pv3):
            ref[...] = jnp.zeros(ref.shape, BF16)

    d3 = DILATION_GROUPS[2][1]
    run3 = TOKEN_TILE // d3

    def gather3(ref, r):
        return jnp.concatenate([ref[j * TOKEN_TILE + r * run3:j * TOKEN_TILE + (r + 1) * run3, :]
                                for j in range(ATTN_TILE // TOKEN_TILE)], axis=0)

    for r in range(d3):
        o, lse = unit(2, gather3(q3, r), cat(gather3(pk3, r), gather3(k3, r)),
                      cat(gather3(pv3, r), gather3(v3, r)), tail_kind)
        o3[pl.ds(r, BLOCK, stride=d3), :] = o
        l3[pl.ds(r, BLOCK, stride=d3), :] = lse

    d2 = DILATION_GROUPS[1][1]
    for n2 in range(ATTN_TILE // TOKEN_TILE):
        for r in range(d2):
            cur = slice(n2 * TOKEN_TILE + r * BLOCK, n2 * TOKEN_TILE + (r + 1) * BLOCK)
            if n2 == 0:
                run = slice(r * BLOCK, (r + 1) * BLOCK)
                k, v, kind = cat(pk2[run, :], k2[cur, :]), cat(pv2[run, :], v2[cur, :]), tail_kind
            else:
                prev = slice(cur.start - TOKEN_TILE, cur.stop - TOKEN_TILE)
                k, v, kind = cat(k2[prev, :], k2[cur, :]), cat(v2[prev, :], v2[cur, :]), full_kind
            o, lse = unit(1, q2[cur, :], k, v, kind)
            o2[pl.ds(n2 * TOKEN_TILE + r, BLOCK, stride=d2), :] = o
            l2[pl.ds(n2 * TOKEN_TILE + r, BLOCK, stride=d2), :] = lse

    for u in range(ATTN_TILE // BLOCK):
        cur = slice(u * BLOCK, (u + 1) * BLOCK)
        if u == 0:
            k, v, kind = cat(pk1[...], k1[cur, :]), cat(pv1[...], v1[cur, :]), tail_kind
        else:
            both = slice((u - 1) * BLOCK, (u + 1) * BLOCK)
            k, v, kind = k1[both, :], v1[both, :], full_kind
        oa, la = unit(0, q1[cur, :], k, v, kind)
        lb, lc = l2[cur, :], l3[cur, :]
        m = jnp.maximum(jnp.maximum(la, lb), lc)
        ea, eb, ec = jnp.exp(la - m), jnp.exp(lb - m), jnp.exp(lc - m)
        mix = (ea * oa + eb * o2[cur, :] + ec * o3[cur, :]) / (ea + eb + ec)
        out_ref[cur, :] = mix.astype(BF16)

    @pl.when(first)
    def _():
        pk1[...] = k1[ATTN_TILE - BLOCK:, :]
        pv1[...] = v1[ATTN_TILE - BLOCK:, :]
        pk2[...] = k2[ATTN_TILE - TOKEN_TILE:, :]
        pv2[...] = v2[ATTN_TILE - TOKEN_TILE:, :]
        pk3[...] = k3[...]
        pv3[...] = v3[...]


def _attention(qkvs, bias, batch, seq):
    t = qkvs[0].shape[0]
    assert seq == 2 * ATTN_TILE, "the previous-tile carry assumes two attention tiles per sequence"
    per_seq = seq // ATTN_TILE

    def part(p):
        return pl.BlockSpec((ATTN_TILE, LANES), lambda b, s, n: (b * per_seq + n, p * SLABS + s))

    in_specs = [part(p) for _ in range(N_GROUPS) for p in range(3)]
    in_specs.append(pl.BlockSpec((N_GROUPS, 2, 2, BLOCK, 2 * BLOCK), lambda b, s, n: (0, s, 0, 0, 0)))
    tails = [BLOCK, BLOCK, TOKEN_TILE, TOKEN_TILE, ATTN_TILE, ATTN_TILE]
    scratch = [pltpu.VMEM((rows, LANES), BF16) for rows in tails]
    scratch += [pltpu.VMEM((ATTN_TILE, LANES), F32)] * (2 * (N_GROUPS - 1))
    est = (2 * 9 * ATTN_TILE * LANES * 2 + 2 * N_GROUPS * 4 * BLOCK * 2 * BLOCK * 4 + sum(tails) * LANES * 2
           + 2 * N_GROUPS * ATTN_TILE * LANES * 4 + 8 * ATTN_TILE * LANES * 4)
    operands = [a for a in qkvs for _ in range(3)]
    return pl.pallas_call(
        _attn_kernel,
        grid=(batch, SLABS, per_seq),
        in_specs=in_specs,
        out_specs=pl.BlockSpec((ATTN_TILE, LANES), lambda b, s, n: (b * per_seq + n, s)),
        out_shape=jax.ShapeDtypeStruct((t, GROUP_WIDTH), BF16),
        scratch_shapes=scratch,
        compiler_params=pltpu.CompilerParams(dimension_semantics=("arbitrary", "arbitrary", "arbitrary"),
                                             vmem_limit_bytes=_vmem_limit(est)),
        name="attn",
    )(*operands, bias)


def _mixout_kernel(x_ref, mod_ref, mix_ref, yc_ref, gate_ref, wao_ref, wco_ref, wo_ref, out_ref, *, d):
    _, _, gate = _mod_rows(mod_ref, 1)
    y_attn = jnp.dot(mix_ref[...], wao_ref[...], preferred_element_type=F32)
    y_conv = jnp.dot(yc_ref[...], wco_ref[...], preferred_element_type=F32)
    g_conv = gate_ref[:, 0:d].astype(F32)
    g_attn = gate_ref[:, d:2 * d].astype(F32)
    merged = _sigmoid(g_conv) * y_conv + _sigmoid(g_attn) * y_attn
    y = jnp.dot(merged.astype(BF16), wo_ref[...], preferred_element_type=F32)
    out_ref[...] = x_ref[...] + gate * y


def _mixout(x, mod, layer, mix, yc, gates, wao, wco, wo, seq, cast=()):
    t, d = x.shape
    tm = FFN_TILE
    per_seq = seq // tm
    row = lambda w: pl.BlockSpec((tm, w), lambda i: (i, 0))
    est = (2 * d * d + GROUP_WIDTH * d) * 2 + 2 * tm * (2 * d * 4 + GROUP_WIDTH * 2 + 3 * d * 2) + tm * d * 4 * 6
    c_in, c_out, c_shapes, c_ops = _casts(cast, t // tm)
    out, *cast_out = pl.pallas_call(
        _with_casts(functools.partial(_mixout_kernel, d=d), 8, 1, len(cast)),
        grid=(t // tm,),
        in_specs=[row(d), pl.BlockSpec((1, 1, N_SUB * 3, d), lambda i: (layer, i // per_seq, 0, 0)),
                  row(GROUP_WIDTH), row(d), row(2 * d),
                  _resident((GROUP_WIDTH, d)), _resident((d, d)), _resident((d, d))] + c_in,
        out_specs=[row(d)] + c_out,
        out_shape=[jax.ShapeDtypeStruct((t, d), F32)] + c_shapes,
        compiler_params=pltpu.CompilerParams(dimension_semantics=("arbitrary",),
                                             vmem_limit_bytes=_vmem_limit(est)),
        name="mixout",
    )(x, mod, mix, yc, gates, wao, wco, wo, *c_ops)
    return out, cast_out


def kernel(x, c, ada_w, ada_b, norm_g, ffn_w_gate, ffn_w_up, ffn_w_down, w_in, conv_w, w_conv_out, w_attn_out,
           w_o, rel_bias, final_g):
    batch, seq, d = x.shape
    depth = ada_w.shape[0]
    assert seq % ATTN_TILE == 0 and ATTN_TILE % TOKEN_TILE == 0 and seq % FFN_TILE == 0
    assert all(win // dil == BLOCK and TOKEN_TILE % (dil * 32) == 0 or dil == 1 for win, dil in DILATION_GROUPS)
    assert w_in.shape[-1] == 3 * QKV_WIDTH + 5 * d and ada_w.shape[-1] == N_SUB * 3 * d

    mod = _ada_mod(c, ada_w, ada_b)
    bias = _t5_bias(rel_bias)
    xt = x.reshape(batch * seq, d)
    fg = final_g.reshape(1, d)
    ffn_w = [w[0, 0].astype(BF16) for w in (ffn_w_gate, ffn_w_up, ffn_w_down)]
    for l in range(depth):
        g = [norm_g[l, j].reshape(1, d) for j in range(N_SUB)]
        xt, (win,) = _ffn(xt, mod, l, 0, g[0], fg, *ffn_w, seq, cast=[(w_in, (l,))])
        (*qkvs, yc, gates), (wao, wco, wo) = _inproj(
            xt, mod, l, g[1], win, conv_w, seq, cast=[(w_attn_out, (l,)), (w_conv_out, (l,)), (w_o, (l,))])
        mix = _attention(qkvs, bias, batch, seq)
        xt, ffn_w = _mixout(xt, mod, l, mix, yc, gates, wao, wco, wo, seq,
                            cast=[(w, (l, 1)) for w in (ffn_w_gate, ffn_w_up, ffn_w_down)])
        nxt = [(w, (l + 1, 0)) for w in (ffn_w_gate, ffn_w_up, ffn_w_down)] if l + 1 < depth else []
        xt, ffn_w = _ffn(xt, mod, l, 2, g[2], fg, *ffn_w, seq, final=l == depth - 1, cast=nxt)
    return xt.reshape(batch, seq, d)
```

```python
import functools

import numpy as np
import jax
import jax.numpy as jnp
from jax import lax
from jax.experimental import pallas as pl
from jax.experimental.pallas import tpu as pltpu

F32 = jnp.float32
BF16 = jnp.bfloat16

HEAD_DIM = 64
HEADS_PER_GROUP = 8
DILATION_GROUPS = ((128, 1), (512, 4), (2048, 16))
N_GROUPS = len(DILATION_GROUPS)
GROUP_WIDTH = HEADS_PER_GROUP * HEAD_DIM
QKV_WIDTH = N_GROUPS * GROUP_WIDTH
NUM_BUCKETS = 32
MAX_DISTANCE = 2048
BLOCK = 128
CONV_K = 3
N_SUB = 3
EPS = 1e-6
LOG2E = float(np.log2(np.e))
LN2 = float(np.log(2.0))

V7X_VMEM_BYTES = 64 * 1024 * 1024
LANES = 128
MXU_TILE = 256
F32_SUBLANES = 8
BF16_SUBLANES = 16

FFN_TILE = 1024
FFN_CHUNKS = 2
TOKEN_TILE = 512
ADA_COL_TILE = 1152
MAX_DILATION = max(d for _, d in DILATION_GROUPS)
ATTN_TILE = BLOCK * MAX_DILATION
SLABS = GROUP_WIDTH // LANES
DEINTERLEAVE_STEP = 4


def _vmem_limit(nbytes):
    return int(min(nbytes * 5 // 4 + (4 << 20), V7X_VMEM_BYTES - (6 << 20)))


def _sigmoid(x):
    return 1.0 / (1.0 + jnp.exp(-x))


def _norm_mod(x, g, shift, scale):
    ms = jnp.mean(x * x, axis=-1, keepdims=True)
    y = x * lax.rsqrt(ms + EPS) * g
    return y * (1.0 + scale) + shift


def _mod_rows(mod_ref, sub):
    base = 3 * sub
    return (mod_ref[0, 0, base:base + 1, :], mod_ref[0, 0, base + 1:base + 2, :],
            mod_ref[0, 0, base + 2:base + 3, :])


def _resident(shape, lead=()):
    index = tuple(lead) + (0,) * len(shape)
    return pl.BlockSpec((None,) * len(lead) + tuple(shape), lambda *_: index, pipeline_mode=pl.Buffered(1))


def _cast_band(rows, steps):
    band = BF16_SUBLANES
    while rows % band or rows // band > steps:
        band += BF16_SUBLANES
    return band


def _casts(sources, steps):
    in_specs, out_specs, out_shapes, operands = [], [], [], []
    for arr, lead in sources:
        rows, cols = arr.shape[-2:]
        band = _cast_band(rows, steps)
        last = rows // band - 1
        in_specs.append(pl.BlockSpec((None,) * len(lead) + (band, cols),
                                     lambda i, lead=lead, last=last: (*lead, jnp.minimum(i, last), 0)))
        out_specs.append(pl.BlockSpec((band, cols), lambda i, last=last: (jnp.minimum(i, last), 0)))
        out_shapes.append(jax.ShapeDtypeStruct((rows, cols), BF16))
        operands.append(arr)
    return in_specs, out_specs, out_shapes, operands


def _with_casts(body, n_in, n_out, n_cast):
    def kernel(*refs):
        ins, rest = refs[:n_in], refs[n_in:]
        cast_in, rest = rest[:n_cast], rest[n_cast:]
        outs, rest = rest[:n_out], rest[n_out:]
        cast_out, scratch = rest[:n_cast], rest[n_cast:]
        for src, dst in zip(cast_in, cast_out):
            dst[...] = src[...].astype(BF16)
        body(*ins, *outs, *scratch)
    return kernel


def _ada_kernel(c_ref, w_ref, b_ref, o_ref):
    c = c_ref[...]
    cs = (c * _sigmoid(c)).astype(BF16)
    o_ref[0] = jnp.dot(cs, w_ref[0].astype(BF16), preferred_element_type=F32) + b_ref[0]


def _ada_mod(c, ada_w, ada_b):
    depth, d, n = ada_w.shape
    b = c.shape[0]
    tn = ADA_COL_TILE
    out = pl.pallas_call(
        _ada_kernel,
        grid=(depth, n // tn),
        in_specs=[pl.BlockSpec((b, d), lambda l, j: (0, 0)),
                  pl.BlockSpec((1, d, tn), lambda l, j: (l, 0, j)),
                  pl.BlockSpec((1, 1, tn), lambda l, j: (l, 0, j))],
        out_specs=pl.BlockSpec((1, b, tn), lambda l, j: (l, 0, j)),
        out_shape=jax.ShapeDtypeStruct((depth, b, n), F32),
        compiler_params=pltpu.CompilerParams(
            dimension_semantics=("arbitrary", "arbitrary"),
            vmem_limit_bytes=_vmem_limit(2 * d * tn * 4 + d * tn * 2)),
        name="ada_mod",
    )(c, ada_w, ada_b.reshape(depth, 1, n))
    return out.reshape(depth, b, N_SUB * 3, d)


def _t5_bucket(dist):
    exact = NUM_BUCKETS // 2
    d = np.maximum(dist, 1).astype(np.float32)
    large = exact + (np.log(d / exact) / np.log(MAX_DISTANCE / exact) * (NUM_BUCKETS - exact)).astype(np.int32)
    large = np.minimum(large, NUM_BUCKETS - 1)
    return np.where(dist < exact, dist, large).astype(np.int32)


def _bucket_maps():
    i = np.arange(BLOCK)[:, None]
    j = np.arange(2 * BLOCK)[None, :]
    rel = np.maximum(i - j + BLOCK, 0)
    return np.stack([_t5_bucket(rel * dil) for _, dil in DILATION_GROUPS]).astype(np.int32)


def _band_valid():
    i = np.arange(BLOCK)[:, None]
    j = np.arange(2 * BLOCK)[None, :]
    in_band = j <= i + BLOCK
    return np.stack([(j >= i) & in_band, (j >= BLOCK) & in_band]).astype(np.int32)


def _bias_kernel(tab_ref, bucket_ref, valid_ref, o_ref):
    g = pl.program_id(0)
    bk = bucket_ref[0]
    for h in range(HEADS_PER_GROUP):
        acc = jnp.zeros(bk.shape, F32)
        for b in range(NUM_BUCKETS):
            acc = jnp.where(bk == b, tab_ref[b, g * HEADS_PER_GROUP + h] * LOG2E, acc)
        for kind in range(2):
            o_ref[0, h, kind] = jnp.where(valid_ref[kind] != 0, acc, -jnp.inf)


def _t5_bias(rel_bias):
    buckets = jnp.asarray(_bucket_maps())
    shape = (N_GROUPS, HEADS_PER_GROUP, 2, BLOCK, 2 * BLOCK)
    return pl.pallas_call(
        _bias_kernel,
        grid=(N_GROUPS,),
        in_specs=[pl.BlockSpec(memory_space=pltpu.SMEM),
                  pl.BlockSpec((1, BLOCK, 2 * BLOCK), lambda g: (g, 0, 0)),
                  pl.BlockSpec((2, BLOCK, 2 * BLOCK), lambda g: (0, 0, 0))],
        out_specs=pl.BlockSpec((1,) + shape[1:], lambda g: (g, 0, 0, 0, 0)),
        out_shape=jax.ShapeDtypeStruct(shape, F32),
        compiler_params=pltpu.CompilerParams(dimension_semantics=("arbitrary",)),
        name="t5_bias",
    )(rel_bias, buckets, jnp.asarray(_band_valid()))


def _ff_chunks(dff):
    tiles = dff // MXU_TILE
    assert tiles * MXU_TILE == dff
    sizes = [tiles // FFN_CHUNKS + (1 if c < tiles % FFN_CHUNKS else 0) for c in range(FFN_CHUNKS)]
    edges = [0]
    for sz in sizes:
        edges.append(edges[-1] + sz * MXU_TILE)
    return list(zip(edges[:-1], edges[1:]))


def _ffn_kernel(x_ref, mod_ref, g_ref, fg_ref, wg_ref, wu_ref, wd_ref, o_ref, *, sub, final):
    x = x_ref[...]
    shift, scale, gate = _mod_rows(mod_ref, sub)
    h = _norm_mod(x, g_ref[...], shift, scale).astype(BF16)
    y = None
    for lo, hi in _ff_chunks(wg_ref.shape[1]):
        a = jnp.dot(h, wg_ref[:, lo:hi], preferred_element_type=F32)
        u = jnp.dot(h, wu_ref[:, lo:hi], preferred_element_type=F32)
        act = (a * _sigmoid(a) * u).astype(BF16)
        part = jnp.dot(act, wd_ref[lo:hi, :], preferred_element_type=F32)
        y = part if y is None else y + part
    out = x + (0.5 * gate) * y
    if final:
        ms = jnp.mean(out * out, axis=-1, keepdims=True)
        out = out * lax.rsqrt(ms + EPS) * fg_ref[...]
    o_ref[...] = out


def _ffn(x, mod, layer, sub, g, final_g, wg, wu, wd, seq, final=False, cast=()):
    t, d = x.shape
    dff = wg.shape[-1]
    tm = FFN_TILE
    per_seq = seq // tm
    steps = t // tm
    c_in, c_out, c_shapes, c_ops = _casts(cast, steps)
    est = 3 * d * dff * 2 + 4 * tm * d * 4 + tm * (dff // FFN_CHUNKS + MXU_TILE) * 10 + tm * d * 10 + (6 << 20)
    body = functools.partial(_ffn_kernel, sub=sub, final=final)
    out, *cast_out = pl.pallas_call(
        _with_casts(body, 7, 1, len(cast)),
        grid=(steps,),
        in_specs=[pl.BlockSpec((tm, d), lambda i: (i, 0)),
                  pl.BlockSpec((1, 1, N_SUB * 3, d), lambda i: (layer, i // per_seq, 0, 0)),
                  _resident((1, d)), _resident((1, d)),
                  _resident((d, dff)), _resident((d, dff)), _resident((dff, d))] + c_in,
        out_specs=[pl.BlockSpec((tm, d), lambda i: (i, 0))] + c_out,
        out_shape=[jax.ShapeDtypeStruct((t, d), F32)] + c_shapes,
        compiler_params=pltpu.CompilerParams(dimension_semantics=("arbitrary",),
                                             vmem_limit_bytes=_vmem_limit(est)),
        name="ffn_final" if final else "ffn",
    )(x, mod, g, final_g, wg, wu, wd, *c_ops)
    return out, cast_out


def _pack_bf16_pair(a, b):
    bits = lambda x: lax.bitcast_convert_type(x.astype(BF16).astype(F32), jnp.uint32)
    return (bits(a) >> 16) | bits(b)


def _unpack_bf16_pair(words):
    low = lax.bitcast_convert_type(words << 16, F32)
    high = lax.bitcast_convert_type(words & jnp.uint32(0xFFFF0000), F32)
    return low.astype(BF16), high.astype(BF16)


def _residue(grun, groups, r):
    low, weight, g = 0, 1, groups
    while g > 1:
        g //= DEINTERLEAVE_STEP
        low += ((grun // g) % DEINTERLEAVE_STEP) * weight
        weight *= DEINTERLEAVE_STEP
    return low + r * weight


def _inproj_kernel(x_ref, mod_ref, g_ref, w_ref, cw_ref, qkv1_ref, qkv2_ref, qkv3_ref, yc_ref,
                   gate_ref, pbuf, ubuf, tbuf, *, per_seq, d):
    tm = x_ref.shape[0]

    @pl.when(pl.program_id(0) % per_seq == 0)
    def _():
        pbuf[0:F32_SUBLANES, :] = jnp.zeros((F32_SUBLANES, d), F32)

    shift, scale, _ = _mod_rows(mod_ref, 1)
    h = _norm_mod(x_ref[...], g_ref[...], shift, scale).astype(BF16)

    def proj(lo, width):
        return jnp.dot(h, w_ref[:, lo:lo + width], preferred_element_type=F32)

    base = 3 * QKV_WIDTH
    p = proj(base + d, d) * proj(base + 2 * d, d)
    pbuf[F32_SUBLANES:F32_SUBLANES + tm, :] = p
    conv = (cw_ref[0:1, :] * pbuf[F32_SUBLANES - 2:F32_SUBLANES - 2 + tm, :]
            + cw_ref[1:2, :] * pbuf[F32_SUBLANES - 1:F32_SUBLANES - 1 + tm, :]
            + cw_ref[2:3, :] * p)
    yc_ref[...] = (proj(base, d) * conv).astype(BF16)
    pbuf[0:F32_SUBLANES, :] = pbuf[tm:tm + F32_SUBLANES, :]

    outs = (qkv1_ref, qkv2_ref, qkv3_ref)
    for part in range(3):
        u = proj(part * QKV_WIDTH, QKV_WIDTH)
        if part == 0:
            u = u * (HEAD_DIM ** -0.5 * LOG2E)
        col = part * GROUP_WIDTH
        for grp, (_, dil) in enumerate(DILATION_GROUPS):
            ug = u[:, grp * GROUP_WIDTH:(grp + 1) * GROUP_WIDTH]
            if dil == 1:
                outs[grp][:, col:col + GROUP_WIDTH] = ug.astype(BF16)
                continue
            buf = ubuf.at[grp - 1]
            for s in range(SLABS // 2):
                pair = [ug[:, (2 * s + e) * LANES:(2 * s + e + 1) * LANES] for e in range(2)]
                buf[s] = _pack_bf16_pair(*pair)
            step, groups = 1, 1
            while step * DEINTERLEAVE_STEP < dil:
                assert buf is not tbuf, "one intermediate buffer: at most two passes"
                run = tm // (groups * DEINTERLEAVE_STEP)
                for s in range(SLABS // 2):
                    for grun in range(groups):
                        for r in range(DEINTERLEAVE_STEP):
                            dst = (grun * DEINTERLEAVE_STEP + r) * run
                            tbuf[s, dst:dst + run, :] = (
                                buf[s, pl.ds(grun * (tm // groups) + r, run, stride=DEINTERLEAVE_STEP), :])
                buf, step, groups = tbuf, step * DEINTERLEAVE_STEP, groups * DEINTERLEAVE_STEP
            last = dil // step
            run = tm // dil
            for s in range(SLABS // 2):
                for grun in range(groups):
                    for r in range(last):
                        res = _residue(grun, groups, r)
                        words = buf[s, pl.ds(grun * (tm // groups) + r, run, stride=last), :]
                        for e, half in enumerate(_unpack_bf16_pair(words)):
                            lo = col + (2 * s + e) * LANES
                            outs[grp][res * run:(res + 1) * run, lo:lo + LANES] = half

    for c in range(2):
        gate_ref[:, c * d:(c + 1) * d] = proj(base + 3 * d + c * d, d).astype(BF16)


def _inproj(x, mod, layer, g, w_in, conv_w, seq, cast=()):
    t, d = x.shape
    n = w_in.shape[-1]
    tm = TOKEN_TILE
    per_seq = seq // tm
    est = (d * n * 2 + 2 * tm * d * 4 + 2 * tm * (3 * QKV_WIDTH + 3 * d) * 2 + (tm + 8) * d * 4
           + N_GROUPS * SLABS * tm * LANES * 4 + tm * d * 4 * 8)
    row = lambda w: pl.BlockSpec((tm, w), lambda i: (i, 0))
    c_in, c_out, c_shapes, c_ops = _casts(cast, t // tm)
    body = functools.partial(_inproj_kernel, per_seq=per_seq, d=d)
    *outs, = pl.pallas_call(
        _with_casts(body, 5, 5, len(cast)),
        grid=(t // tm,),
        in_specs=[row(d),
                  pl.BlockSpec((1, 1, N_SUB * 3, d), lambda i: (layer, i // per_seq, 0, 0)),
                  _resident((1, d)), _resident((d, n)), _resident((CONV_K, d), (layer,))] + c_in,
        out_specs=[row(QKV_WIDTH), row(QKV_WIDTH), row(QKV_WIDTH), row(d), row(2 * d)] + c_out,
        out_shape=[jax.ShapeDtypeStruct((t, QKV_WIDTH), BF16)] * 3
                  + [jax.ShapeDtypeStruct((t, d), BF16), jax.ShapeDtypeStruct((t, 2 * d), BF16)] + c_shapes,
        scratch_shapes=[pltpu.VMEM((tm + F32_SUBLANES, d), F32),
                        pltpu.VMEM((N_GROUPS - 1, SLABS // 2, tm, LANES), jnp.uint32),
                        pltpu.VMEM((SLABS // 2, tm, LANES), jnp.uint32)],
        compiler_params=pltpu.CompilerParams(dimension_semantics=("arbitrary",),
                                             vmem_limit_bytes=_vmem_limit(est)),
        name="inproj",
    )(x, mod, g, w_in, conv_w, *c_ops)
    return outs[:5], outs[5:]


def _attn_kernel(q1, k1, v1, q2, k2, v2, q3, k3, v3, bias_ref, out_ref,
                 pk1, pv1, pk2, pv2, pk3, pv3, o2, l2, o3, l3):
    n = pl.program_id(2)
    lane = lax.broadcasted_iota(jnp.int32, (1, LANES), 1)
    low = lane < HEAD_DIM
    keeps = (low.astype(BF16), 1 - low.astype(BF16))
    nt = (((1,), (1,)), ((), ()))

    def unit(grp, q, k, v, kind):
        v_ext = jnp.concatenate([v, jnp.ones((2 * BLOCK, LANES), BF16)], axis=1)
        parts = []
        for half in range(2):
            s = lax.dot_general(q * keeps[half], k, nt, preferred_element_type=F32)
            s = s + bias_ref[grp, half, kind]
            m = jnp.max(s, axis=-1, keepdims=True)
            p = jnp.exp2(s - m).astype(BF16)
            oe = jnp.dot(p, v_ext, preferred_element_type=F32)
            parts.append((oe[:, :LANES], oe[:, LANES:], m))
        (oa, la, ma), (ob, lb, mb) = parts
        l = jnp.where(low, la, lb)
        return jnp.where(low, oa, ob) / l, jnp.where(low, ma, mb) * LN2 + jnp.log(l)

    def cat(a, b):
        return jnp.concatenate([a, b], axis=0)

    first = n == 0
    tail_kind = first.astype(jnp.int32)
    full_kind = 0

    @pl.when(first)
    def _():
        for ref in (pk1, pv1, pk2, pv2, pk3, pv3):
            ref[...] = jnp.zeros(ref.shape, BF16)

    d3 = DILATION_GROUPS[2][1]
    run3 = TOKEN_TILE // d3

    def gather3(ref, r):
        return jnp.concatenate([ref[j * TOKEN_TILE + r * run3:j * TOKEN_TILE + (r + 1) * run3, :]
                                for j in range(ATTN_TILE // TOKEN_TILE)], axis=0)

    for r in range(d3):
        o, lse = unit(2, gather3(q3, r), cat(gather3(pk3, r), gather3(k3, r)),
                      cat(gather3(pv3, r), gather3(v3, r)), tail_kind)
        o3[pl.ds(r, BLOCK, stride=d3), :] = o
        l3[pl.ds(r, BLOCK, stride=d3), :] = lse

    d2 = DILATION_GROUPS[1][1]
    for n2 in range(ATTN_TILE // TOKEN_TILE):
        for r in range(d2):
            cur = slice(n2 * TOKEN_TILE + r * BLOCK, n2 * TOKEN_TILE + (r + 1) * BLOCK)
            if n2 == 0:
                run = slice(r * BLOCK, (r + 1) * BLOCK)
                k, v, kind = cat(pk2[run, :], k2[cur, :]), cat(pv2[run, :], v2[cur, :]), tail_kind
            else:
                prev = slice(cur.start - TOKEN_TILE, cur.stop - TOKEN_TILE)
                k, v, kind = cat(k2[prev, :], k2[cur, :]), cat(v2[prev, :], v2[cur, :]), full_kind
            o, lse = unit(1, q2[cur, :], k, v, kind)
            o2[pl.ds(n2 * TOKEN_TILE + r, BLOCK, stride=d2), :] = o
            l2[pl.ds(n2 * TOKEN_TILE + r, BLOCK, stride=d2), :] = lse

    for u in range(ATTN_TILE // BLOCK):
        cur = slice(u * BLOCK, (u + 1) * BLOCK)
        if u == 0:
            k, v, kind = cat(pk1[...], k1[cur, :]), cat(pv1[...], v1[cur, :]), tail_kind
        else:
            both = slice((u - 1) * BLOCK, (u + 1) * BLOCK)
            k, v, kind = k1[both, :], v1[both, :], full_kind
        oa, la = unit(0, q1[cur, :], k, v, kind)
        lb, lc = l2[cur, :], l3[cur, :]
        m = jnp.maximum(jnp.maximum(la, lb), lc)
        ea, eb, ec = jnp.exp(la - m), jnp.exp(lb - m), jnp.exp(lc - m)
        mix = (ea * oa + eb * o2[cur, :] + ec * o3[cur, :]) / (ea + eb + ec)
        out_ref[cur, :] = mix.astype(BF16)

    @pl.when(first)
    def _():
        pk1[...] = k1[ATTN_TILE - BLOCK:, :]
        pv1[...] = v1[ATTN_TILE - BLOCK:, :]
        pk2[...] = k2[ATTN_TILE - TOKEN_TILE:, :]
        pv2[...] = v2[ATTN_TILE - TOKEN_TILE:, :]
        pk3[...] = k3[...]
        pv3[...] = v3[...]


def _attention(qkvs, bias, batch, seq):
    t = qkvs[0].shape[0]
    assert seq == 2 * ATTN_TILE, "the previous-tile carry assumes two attention tiles per sequence"
    per_seq = seq // ATTN_TILE

    def part(p):
        return pl.BlockSpec((ATTN_TILE, LANES), lambda b, s, n: (b * per_seq + n, p * SLABS + s))

    in_specs = [part(p) for _ in range(N_GROUPS) for p in range(3)]
    in_specs.append(pl.BlockSpec((N_GROUPS, 2, 2, BLOCK, 2 * BLOCK), lambda b, s, n: (0, s, 0, 0, 0)))
    tails = [BLOCK, BLOCK, TOKEN_TILE, TOKEN_TILE, ATTN_TILE, ATTN_TILE]
    scratch = [pltpu.VMEM((rows, LANES), BF16) for rows in tails]
    scratch += [pltpu.VMEM((ATTN_TILE, LANES), F32)] * (2 * (N_GROUPS - 1))
    est = (2 * 9 * ATTN_TILE * LANES * 2 + 2 * N_GROUPS * 4 * BLOCK * 2 * BLOCK * 4 + sum(tails) * LANES * 2
           + 2 * N_GROUPS * ATTN_TILE * LANES * 4 + 8 * ATTN_TILE * LANES * 4)
    operands = [a for a in qkvs for _ in range(3)]
    return pl.pallas_call(
        _attn_kernel,
        grid=(batch, SLABS, per_seq),
        in_specs=in_specs,
        out_specs=pl.BlockSpec((ATTN_TILE, LANES), lambda b, s, n: (b * per_seq + n, s)),
        out_shape=jax.ShapeDtypeStruct((t, GROUP_WIDTH), BF16),
        scratch_shapes=scratch,
        compiler_params=pltpu.CompilerParams(dimension_semantics=("arbitrary", "arbitrary", "arbitrary"),
                                             vmem_limit_bytes=_vmem_limit(est)),
        name="attn",
    )(*operands, bias)


def _mixout_kernel(x_ref, mod_ref, mix_ref, yc_ref, gate_ref, wao_ref, wco_ref, wo_ref, out_ref, *, d):
    _, _, gate = _mod_rows(mod_ref, 1)
    y_attn = jnp.dot(mix_ref[...], wao_ref[...], preferred_element_type=F32)
    y_conv = jnp.dot(yc_ref[...], wco_ref[...], preferred_element_type=F32)
    g_conv = gate_ref[:, 0:d].astype(F32)
    g_attn = gate_ref[:, d:2 * d].astype(F32)
    merged = _sigmoid(g_conv) * y_conv + _sigmoid(g_attn) * y_attn
    y = jnp.dot(merged.astype(BF16), wo_ref[...], preferred_element_type=F32)
    out_ref[...] = x_ref[...] + gate * y


def _mixout(x, mod, layer, mix, yc, gates, wao, wco, wo, seq, cast=()):
    t, d = x.shape
    tm = FFN_TILE
    per_seq = seq // tm
    row = lambda w: pl.BlockSpec((tm, w), lambda i: (i, 0))
    est = (2 * d * d + GROUP_WIDTH * d) * 2 + 2 * tm * (2 * d * 4 + GROUP_WIDTH * 2 + 3 * d * 2) + tm * d * 4 * 6
    c_in, c_out, c_shapes, c_ops = _casts(cast, t // tm)
    out, *cast_out = pl.pallas_call(
        _with_casts(functools.partial(_mixout_kernel, d=d), 8, 1, len(cast)),
        grid=(t // tm,),
        in_specs=[row(d), pl.BlockSpec((1, 1, N_SUB * 3, d), lambda i: (layer, i // per_seq, 0, 0)),
                  row(GROUP_WIDTH), row(d), row(2 * d),
                  _resident((GROUP_WIDTH, d)), _resident((d, d)), _resident((d, d))] + c_in,
        out_specs=[row(d)] + c_out,
        out_shape=[jax.ShapeDtypeStruct((t, d), F32)] + c_shapes,
        compiler_params=pltpu.CompilerParams(dimension_semantics=("arbitrary",),
                                             vmem_limit_bytes=_vmem_limit(est)),
        name="mixout",
    )(x, mod, mix, yc, gates, wao, wco, wo, *c_ops)
    return out, cast_out


def kernel(x, c, ada_w, ada_b, norm_g, ffn_w_gate, ffn_w_up, ffn_w_down, w_in, conv_w, w_conv_out, w_attn_out,
           w_o, rel_bias, final_g):
    batch, seq, d = x.shape
    depth = ada_w.shape[0]
    assert seq % ATTN_TILE == 0 and ATTN_TILE % TOKEN_TILE == 0 and seq % FFN_TILE == 0
    assert all(win // dil == BLOCK and TOKEN_TILE % (dil * 32) == 0 or dil == 1 for win, dil in DILATION_GROUPS)
    assert w_in.shape[-1] == 3 * QKV_WIDTH + 5 * d and ada_w.shape[-1] == N_SUB * 3 * d

    mod = _ada_mod(c, ada_w, ada_b)
    bias = _t5_bias(rel_bias)
    xt = x.reshape(batch * seq, d)
    fg = final_g.reshape(1, d)
    ffn_w = [w[0, 0].astype(BF16) for w in (ffn_w_gate, ffn_w_up, ffn_w_down)]
    for l in range(depth):
        g = [norm_g[l, j].reshape(1, d) for j in range(N_SUB)]
        xt, (win,) = _ffn(xt, mod, l, 0, g[0], fg, *ffn_w, seq, cast=[(w_in, (l,))])
        (*qkvs, yc, gates), (wao, wco, wo) = _inproj(
            xt, mod, l, g[1], win, conv_w, seq, cast=[(w_attn_out, (l,)), (w_conv_out, (l,)), (w_o, (l,))])
        mix = _attention(qkvs, bias, batch, seq)
        xt, ffn_w = _mixout(xt, mod, l, mix, yc, gates, wao, wco, wo, seq,
                            cast=[(w, (l, 1)) for w in (ffn_w_gate, ffn_w_up, ffn_w_down)])
        nxt = [(w, (l + 1, 0)) for w in (ffn_w_gate, ffn_w_up, ffn_w_down)] if l + 1 < depth else []
        xt, ffn_w = _ffn(xt, mod, l, 2, g[2], fg, *ffn_w, seq, final=l == depth - 1, cast=nxt)
    return xt.reshape(batch, seq, d)
```

```python
import functools

import numpy as np
import jax
import jax.numpy as jnp
from jax import lax
from jax.experimental import pallas as pl
from jax.experimental.pallas import tpu as pltpu

F32 = jnp.float32
BF16 = jnp.bfloat16

HEAD_DIM = 64
HEADS_PER_GROUP = 8
DILATION_GROUPS = ((128, 1), (512, 4), (2048, 16))
N_GROUPS = len(DILATION_GROUPS)
GROUP_WIDTH = HEADS_PER_GROUP * HEAD_DIM
QKV_WIDTH = N_GROUPS * GROUP_WIDTH
NUM_BUCKETS = 32
MAX_DISTANCE = 2048
BLOCK = 128
CONV_K = 3
N_SUB = 3
EPS = 1e-6
LOG2E = float(np.log2(np.e))
LN2 = float(np.log(2.0))

V7X_VMEM_BYTES = 64 * 1024 * 1024
LANES = 128
MXU_TILE = 256
F32_SUBLANES = 8
BF16_SUBLANES = 16

FFN_TILE = 1024
FFN_CHUNKS = 2
TOKEN_TILE = 512
ADA_COL_TILE = 1152
MAX_DILATION = max(d for _, d in DILATION_GROUPS)
ATTN_TILE = BLOCK * MAX_DILATION
SLABS = GROUP_WIDTH // LANES
ATTN_SLABS = 2
DEINTERLEAVE_STEP = 4


def _vmem_limit(nbytes):
    return int(min(nbytes * 5 // 4 + (4 << 20), V7X_VMEM_BYTES - (6 << 20)))


def _sigmoid(x):
    return 1.0 / (1.0 + jnp.exp(-x))


def _norm_mod(x, g, shift, scale):
    ms = jnp.mean(x * x, axis=-1, keepdims=True)
    y = x * lax.rsqrt(ms + EPS) * g
    return y * (1.0 + scale) + shift


def _mod_rows(mod_ref, sub):
    base = 3 * sub
    return (mod_ref[0, 0, base:base + 1, :], mod_ref[0, 0, base + 1:base + 2, :],
            mod_ref[0, 0, base + 2:base + 3, :])


def _resident(shape, lead=()):
    index = tuple(lead) + (0,) * len(shape)
    return pl.BlockSpec((None,) * len(lead) + tuple(shape), lambda *_: index, pipeline_mode=pl.Buffered(1))


def _cast_band(rows, steps):
    band = BF16_SUBLANES
    while rows % band or rows // band > steps:
        band += BF16_SUBLANES
    return band


def _casts(sources, steps):
    in_specs, out_specs, out_shapes, operands = [], [], [], []
    for arr, lead in sources:
        rows, cols = arr.shape[-2:]
        band = _cast_band(rows, steps)
        last = rows // band - 1
        in_specs.append(pl.BlockSpec((None,) * len(lead) + (band, cols),
                                     lambda i, lead=lead, last=last: (*lead, jnp.minimum(i, last), 0)))
        out_specs.append(pl.BlockSpec((band, cols), lambda i, last=last: (jnp.minimum(i, last), 0)))
        out_shapes.append(jax.ShapeDtypeStruct((rows, cols), BF16))
        operands.append(arr)
    return in_specs, out_specs, out_shapes, operands


def _with_casts(body, n_in, n_out, n_cast):
    def kernel(*refs):
        ins, rest = refs[:n_in], refs[n_in:]
        cast_in, rest = rest[:n_cast], rest[n_cast:]
        outs, rest = rest[:n_out], rest[n_out:]
        cast_out, scratch = rest[:n_cast], rest[n_cast:]
        for src, dst in zip(cast_in, cast_out):
            dst[...] = src[...].astype(BF16)
        body(*ins, *outs, *scratch)
    return kernel


def _ada_kernel(c_ref, w_ref, b_ref, o_ref):
    c = c_ref[...]
    cs = (c * _sigmoid(c)).astype(BF16)
    o_ref[0] = jnp.dot(cs, w_ref[0].astype(BF16), preferred_element_type=F32) + b_ref[0]


def _ada_mod(c, ada_w, ada_b):
    depth, d, n = ada_w.shape
    b = c.shape[0]
    tn = ADA_COL_TILE
    out = pl.pallas_call(
        _ada_kernel,
        grid=(depth, n // tn),
        in_specs=[pl.BlockSpec((b, d), lambda l, j: (0, 0)),
                  pl.BlockSpec((1, d, tn), lambda l, j: (l, 0, j)),
                  pl.BlockSpec((1, 1, tn), lambda l, j: (l, 0, j))],
        out_specs=pl.BlockSpec((1, b, tn), lambda l, j: (l, 0, j)),
        out_shape=jax.ShapeDtypeStruct((depth, b, n), F32),
        compiler_params=pltpu.CompilerParams(
            dimension_semantics=("arbitrary", "arbitrary"),
            vmem_limit_bytes=_vmem_limit(2 * d * tn * 4 + d * tn * 2)),
        name="ada_mod",
    )(c, ada_w, ada_b.reshape(depth, 1, n))
    return out.reshape(depth, b, N_SUB * 3, d)


def _t5_bucket(dist):
    exact = NUM_BUCKETS // 2
    d = np.maximum(dist, 1).astype(np.float32)
    large = exact + (np.log(d / exact) / np.log(MAX_DISTANCE / exact) * (NUM_BUCKETS - exact)).astype(np.int32)
    large = np.minimum(large, NUM_BUCKETS - 1)
    return np.where(dist < exact, dist, large).astype(np.int32)


def _bucket_maps():
    i = np.arange(BLOCK)[:, None]
    j = np.arange(2 * BLOCK)[None, :]
    rel = np.maximum(i - j + BLOCK, 0)
    return np.stack([_t5_bucket(rel * dil) for _, dil in DILATION_GROUPS]).astype(np.int32)


def _band_valid():
    i = np.arange(BLOCK)[:, None]
    j = np.arange(2 * BLOCK)[None, :]
    in_band = j <= i + BLOCK
    return np.stack([(j >= i) & in_band, (j >= BLOCK) & in_band]).astype(np.int32)


def _bias_kernel(tab_ref, bucket_ref, valid_ref, o_ref):
    g = pl.program_id(0)
    bk = bucket_ref[0]
    for h in range(HEADS_PER_GROUP):
        acc = jnp.zeros(bk.shape, F32)
        for b in range(NUM_BUCKETS):
            acc = jnp.where(bk == b, tab_ref[b, g * HEADS_PER_GROUP + h] * LOG2E, acc)
        for kind in range(2):
            o_ref[0, h, kind] = jnp.where(valid_ref[kind] != 0, acc, -jnp.inf)


def _t5_bias(rel_bias):
    buckets = jnp.asarray(_bucket_maps())
    shape = (N_GROUPS, HEADS_PER_GROUP, 2, BLOCK, 2 * BLOCK)
    return pl.pallas_call(
        _bias_kernel,
        grid=(N_GROUPS,),
        in_specs=[pl.BlockSpec(memory_space=pltpu.SMEM),
                  pl.BlockSpec((1, BLOCK, 2 * BLOCK), lambda g: (g, 0, 0)),
                  pl.BlockSpec((2, BLOCK, 2 * BLOCK), lambda g: (0, 0, 0))],
        out_specs=pl.BlockSpec((1,) + shape[1:], lambda g: (g, 0, 0, 0, 0)),
        out_shape=jax.ShapeDtypeStruct(shape, F32),
        compiler_params=pltpu.CompilerParams(dimension_semantics=("arbitrary",)),
        name="t5_bias",
    )(rel_bias, buckets, jnp.asarray(_band_valid()))


def _ff_chunks(dff):
    tiles = dff // MXU_TILE
    assert tiles * MXU_TILE == dff
    sizes = [tiles // FFN_CHUNKS + (1 if c < tiles % FFN_CHUNKS else 0) for c in range(FFN_CHUNKS)]
    edges = [0]
    for sz in sizes:
        edges.append(edges[-1] + sz * MXU_TILE)
    return list(zip(edges[:-1], edges[1:]))


def _ffn_kernel(x_ref, mod_ref, g_ref, fg_ref, wg_ref, wu_ref, wd_ref, o_ref, *, sub, final):
    x = x_ref[...]
    shift, scale, gate = _mod_rows(mod_ref, sub)
    h = _norm_mod(x, g_ref[...], shift, scale).astype(BF16)
    y = None
    for lo, hi in _ff_chunks(wg_ref.shape[1]):
        a = jnp.dot(h, wg_ref[:, lo:hi], preferred_element_type=F32)
        u = jnp.dot(h, wu_ref[:, lo:hi], preferred_element_type=F32)
        act = (a * _sigmoid(a) * u).astype(BF16)
        part = jnp.dot(act, wd_ref[lo:hi, :], preferred_element_type=F32)
        y = part if y is None else y + part
    out = x + (0.5 * gate) * y
    if final:
        ms = jnp.mean(out * out, axis=-1, keepdims=True)
        out = out * lax.rsqrt(ms + EPS) * fg_ref[...]
    o_ref[...] = out


def _ffn(x, mod, layer, sub, g, final_g, wg, wu, wd, seq, final=False, cast=()):
    t, d = x.shape
    dff = wg.shape[-1]
    tm = FFN_TILE
    per_seq = seq // tm
    steps = t // tm
    c_in, c_out, c_shapes, c_ops = _casts(cast, steps)
    est = 3 * d * dff * 2 + 4 * tm * d * 4 + tm * (dff // FFN_CHUNKS + MXU_TILE) * 10 + tm * d * 10 + (6 << 20)
    body = functools.partial(_ffn_kernel, sub=sub, final=final)
    out, *cast_out = pl.pallas_call(
        _with_casts(body, 7, 1, len(cast)),
        grid=(steps,),
        in_specs=[pl.BlockSpec((tm, d), lambda i: (i, 0)),
                  pl.BlockSpec((1, 1, N_SUB * 3, d), lambda i: (layer, i // per_seq, 0, 0)),
                  _resident((1, d)), _resident((1, d)),
                  _resident((d, dff)), _resident((d, dff)), _resident((dff, d))] + c_in,
        out_specs=[pl.BlockSpec((tm, d), lambda i: (i, 0))] + c_out,
        out_shape=[jax.ShapeDtypeStruct((t, d), F32)] + c_shapes,
        compiler_params=pltpu.CompilerParams(dimension_semantics=("arbitrary",),
                                             vmem_limit_bytes=_vmem_limit(est)),
        name="ffn_final" if final else "ffn",
    )(x, mod, g, final_g, wg, wu, wd, *c_ops)
    return out, cast_out


def _pack_bf16_pair(a, b):
    bits = lambda x: lax.bitcast_convert_type(x.astype(BF16).astype(F32), jnp.uint32)
    return (bits(a) >> 16) | bits(b)


def _unpack_bf16_pair(words):
    low = lax.bitcast_convert_type(words << 16, F32)
    high = lax.bitcast_convert_type(words & jnp.uint32(0xFFFF0000), F32)
    return low.astype(BF16), high.astype(BF16)


def _residue(grun, groups, r):
    low, weight, g = 0, 1, groups
    while g > 1:
        g //= DEINTERLEAVE_STEP
        low += ((grun // g) % DEINTERLEAVE_STEP) * weight
        weight *= DEINTERLEAVE_STEP
    return low + r * weight


def _inproj_kernel(x_ref, mod_ref, g_ref, w_ref, cw_ref, qkv1_ref, qkv2_ref, qkv3_ref, yc_ref,
                   gate_ref, pbuf, ubuf, tbuf, *, per_seq, d):
    tm = x_ref.shape[0]

    @pl.when(pl.program_id(0) % per_seq == 0)
    def _():
        pbuf[0:F32_SUBLANES, :] = jnp.zeros((F32_SUBLANES, d), F32)

    shift, scale, _ = _mod_rows(mod_ref, 1)
    h = _norm_mod(x_ref[...], g_ref[...], shift, scale).astype(BF16)

    def proj(lo, width):
        return jnp.dot(h, w_ref[:, lo:lo + width], preferred_element_type=F32)

    base = 3 * QKV_WIDTH
    p = proj(base + d, d) * proj(base + 2 * d, d)
    pbuf[F32_SUBLANES:F32_SUBLANES + tm, :] = p
    conv = (cw_ref[0:1, :] * pbuf[F32_SUBLANES - 2:F32_SUBLANES - 2 + tm, :]
            + cw_ref[1:2, :] * pbuf[F32_SUBLANES - 1:F32_SUBLANES - 1 + tm, :]
            + cw_ref[2:3, :] * p)
    yc_ref[...] = (proj(base, d) * conv).astype(BF16)
    pbuf[0:F32_SUBLANES, :] = pbuf[tm:tm + F32_SUBLANES, :]

    outs = (qkv1_ref, qkv2_ref, qkv3_ref)
    for part in range(3):
        u = proj(part * QKV_WIDTH, QKV_WIDTH)
        if part == 0:
            u = u * (HEAD_DIM ** -0.5 * LOG2E)
        col = part * GROUP_WIDTH
        for grp, (_, dil) in enumerate(DILATION_GROUPS):
            ug = u[:, grp * GROUP_WIDTH:(grp + 1) * GROUP_WIDTH]
            if dil == 1:
                outs[grp][:, col:col + GROUP_WIDTH] = ug.astype(BF16)
                continue
            buf = ubuf.at[grp - 1]
            for s in range(SLABS // 2):
                pair = [ug[:, (2 * s + e) * LANES:(2 * s + e + 1) * LANES] for e in range(2)]
                buf[s] = _pack_bf16_pair(*pair)
            step, groups = 1, 1
            while step * DEINTERLEAVE_STEP < dil:
                assert buf is not tbuf, "one intermediate buffer: at most two passes"
                run = tm // (groups * DEINTERLEAVE_STEP)
                for s in range(SLABS // 2):
                    for grun in range(groups):
                        for r in range(DEINTERLEAVE_STEP):
                            dst = (grun * DEINTERLEAVE_STEP + r) * run
                            tbuf[s, dst:dst + run, :] = (
                                buf[s, pl.ds(grun * (tm // groups) + r, run, stride=DEINTERLEAVE_STEP), :])
                buf, step, groups = tbuf, step * DEINTERLEAVE_STEP, groups * DEINTERLEAVE_STEP
            last = dil // step
            run = tm // dil
            for s in range(SLABS // 2):
                for grun in range(groups):
                    for r in range(last):
                        res = _residue(grun, groups, r)
                        words = buf[s, pl.ds(grun * (tm // groups) + r, run, stride=last), :]
                        for e, half in enumerate(_unpack_bf16_pair(words)):
                            lo = col + (2 * s + e) * LANES
                            outs[grp][res * run:(res + 1) * run, lo:lo + LANES] = half

    for c in range(2):
        gate_ref[:, c * d:(c + 1) * d] = proj(base + 3 * d + c * d, d).astype(BF16)


def _inproj(x, mod, layer, g, w_in, conv_w, seq, cast=()):
    t, d = x.shape
    n = w_in.shape[-1]
    tm = TOKEN_TILE
    per_seq = seq // tm
    est = (d * n * 2 + 2 * tm * d * 4 + 2 * tm * (3 * QKV_WIDTH + 3 * d) * 2 + (tm + 8) * d * 4
           + N_GROUPS * SLABS * tm * LANES * 4 + tm * d * 4 * 8)
    row = lambda w: pl.BlockSpec((tm, w), lambda i: (i, 0))
    c_in, c_out, c_shapes, c_ops = _casts(cast, t // tm)
    body = functools.partial(_inproj_kernel, per_seq=per_seq, d=d)
    *outs, = pl.pallas_call(
        _with_casts(body, 5, 5, len(cast)),
        grid=(t // tm,),
        in_specs=[row(d),
                  pl.BlockSpec((1, 1, N_SUB * 3, d), lambda i: (layer, i // per_seq, 0, 0)),
                  _resident((1, d)), _resident((d, n)), _resident((CONV_K, d), (layer,))] + c_in,
        out_specs=[row(QKV_WIDTH), row(QKV_WIDTH), row(QKV_WIDTH), row(d), row(2 * d)] + c_out,
        out_shape=[jax.ShapeDtypeStruct((t, QKV_WIDTH), BF16)] * 3
                  + [jax.ShapeDtypeStruct((t, d), BF16), jax.ShapeDtypeStruct((t, 2 * d), BF16)] + c_shapes,
        scratch_shapes=[pltpu.VMEM((tm + F32_SUBLANES, d), F32),
                        pltpu.VMEM((N_GROUPS - 1, SLABS // 2, tm, LANES), jnp.uint32),
                        pltpu.VMEM((SLABS // 2, tm, LANES), jnp.uint32)],
        compiler_params=pltpu.CompilerParams(dimension_semantics=("arbitrary",),
                                             vmem_limit_bytes=_vmem_limit(est)),
        name="inproj",
    )(x, mod, g, w_in, conv_w, *c_ops)
    return outs[:5], outs[5:]


def _attn_kernel(q1, k1, v1, q2, k2, v2, q3, k3, v3, bias_ref, out_ref,
                 pk1, pv1, pk2, pv2, pk3, pv3, o2, l2, o3, l3):
    n = pl.program_id(2)
    lane = lax.broadcasted_iota(jnp.int32, (1, LANES), 1)
    low = lane < HEAD_DIM
    keeps = (low.astype(BF16), 1 - low.astype(BF16))
    nt = (((1,), (1,)), ((), ()))

    def unit(grp, pair, q, k, v, kind):
        v_ext = jnp.concatenate([v, jnp.ones((2 * BLOCK, LANES), BF16)], axis=1)
        parts = []
        for half in range(2):
            s = lax.dot_general(q * keeps[half], k, nt, preferred_element_type=F32)
            s = s + bias_ref[grp, 2 * pair + half, kind]
            m = jnp.max(s, axis=-1, keepdims=True)
            p = jnp.exp2(s - m).astype(BF16)
            oe = jnp.dot(p, v_ext, preferred_element_type=F32)
            parts.append((oe[:, :LANES], oe[:, LANES:], m))
        (oa, la, ma), (ob, lb, mb) = parts
        l = jnp.where(low, la, lb)
        return jnp.where(low, oa, ob) / l, jnp.where(low, ma, mb) * LN2 + jnp.log(l)

    def cat(a, b):
        return jnp.concatenate([a, b], axis=0)

    first = n == 0
    tail_kind = first.astype(jnp.int32)
    full_kind = 0

    @pl.when(first)
    def _():
        for ref in (pk1, pv1, pk2, pv2, pk3, pv3):
            ref[...] = jnp.zeros(ref.shape, BF16)

    d2 = DILATION_GROUPS[1][1]
    d3 = DILATION_GROUPS[2][1]
    run3 = TOKEN_TILE // d3

    for pair in range(ATTN_SLABS):
        ls = slice(pair * LANES, (pair + 1) * LANES)

        def gather3(ref, r):
            return jnp.concatenate([ref[j * TOKEN_TILE + r * run3:j * TOKEN_TILE + (r + 1) * run3, ls]
                                    for j in range(ATTN_TILE // TOKEN_TILE)], axis=0)

        for r in range(d3):
            o, lse = unit(2, pair, gather3(q3, r), cat(gather3(pk3, r), gather3(k3, r)),
                          cat(gather3(pv3, r), gather3(v3, r)), tail_kind)
            o3[pair, pl.ds(r, BLOCK, stride=d3), :] = o
            l3[pair, pl.ds(r, BLOCK, stride=d3), :] = lse

        for n2 in range(ATTN_TILE // TOKEN_TILE):
            for r in range(d2):
                cur = slice(n2 * TOKEN_TILE + r * BLOCK, n2 * TOKEN_TILE + (r + 1) * BLOCK)
                if n2 == 0:
                    run = slice(r * BLOCK, (r + 1) * BLOCK)
                    k, v, kind = cat(pk2[run, ls], k2[cur, ls]), cat(pv2[run, ls], v2[cur, ls]), tail_kind
                else:
                    prev = slice(cur.start - TOKEN_TILE, cur.stop - TOKEN_TILE)
                    k, v, kind = cat(k2[prev, ls], k2[cur, ls]), cat(v2[prev, ls], v2[cur, ls]), full_kind
                o, lse = unit(1, pair, q2[cur, ls], k, v, kind)
                o2[pair, pl.ds(n2 * TOKEN_TILE + r, BLOCK, stride=d2), :] = o
                l2[pair, pl.ds(n2 * TOKEN_TILE + r, BLOCK, stride=d2), :] = lse

        for u in range(ATTN_TILE // BLOCK):
            cur = slice(u * BLOCK, (u + 1) * BLOCK)
            if u == 0:
                k, v, kind = cat(pk1[:, ls], k1[cur, ls]), cat(pv1[:, ls], v1[cur, ls]), tail_kind
            else:
                both = slice((u - 1) * BLOCK, (u + 1) * BLOCK)
                k, v, kind = k1[both, ls], v1[both, ls], full_kind
            oa, la = unit(0, pair, q1[cur, ls], k, v, kind)
            lb, lc = l2[pair, cur, :], l3[pair, cur, :]
            m = jnp.maximum(jnp.maximum(la, lb), lc)
            ea, eb, ec = jnp.exp(la - m), jnp.exp(lb - m), jnp.exp(lc - m)
            mix = (ea * oa + eb * o2[pair, cur, :] + ec * o3[pair, cur, :]) / (ea + eb + ec)
            out_ref[cur, ls] = mix.astype(BF16)

    @pl.when(first)
    def _():
        pk1[...] = k1[ATTN_TILE - BLOCK:, :]
        pv1[...] = v1[ATTN_TILE - BLOCK:, :]
        pk2[...] = k2[ATTN_TILE - TOKEN_TILE:, :]
        pv2[...] = v2[ATTN_TILE - TOKEN_TILE:, :]
        pk3[...] = k3[...]
        pv3[...] = v3[...]


def _attention(qkvs, bias, batch, seq):
    t = qkvs[0].shape[0]
    assert seq == 2 * ATTN_TILE, "the previous-tile carry assumes two attention tiles per sequence"
    per_seq = seq // ATTN_TILE
    width = ATTN_SLABS * LANES
    steps = SLABS // ATTN_SLABS

    def part(p):
        return pl.BlockSpec((ATTN_TILE, width), lambda b, s, n: (b * per_seq + n, p * steps + s))

    in_specs = [part(p) for _ in range(N_GROUPS) for p in range(3)]
    in_specs.append(pl.BlockSpec((N_GROUPS, 2 * ATTN_SLABS, 2, BLOCK, 2 * BLOCK), lambda b, s, n: (0, s, 0, 0, 0)))
    tails = [BLOCK, BLOCK, TOKEN_TILE, TOKEN_TILE, ATTN_TILE, ATTN_TILE]
    scratch = [pltpu.VMEM((rows, width), BF16) for rows in tails]
    scratch += [pltpu.VMEM((ATTN_SLABS, ATTN_TILE, LANES), F32)] * (2 * (N_GROUPS - 1))
    est = (2 * 9 * ATTN_TILE * width * 2 + 2 * N_GROUPS * 4 * ATTN_SLABS * BLOCK * 2 * BLOCK * 4
           + sum(tails) * width * 2 + 2 * (N_GROUPS - 1) * ATTN_SLABS * ATTN_TILE * LANES * 4
           + 8 * ATTN_TILE * LANES * 4)
    operands = [a for a in qkvs for _ in range(3)]
    return pl.pallas_call(
        _attn_kernel,
        grid=(batch, steps, per_seq),
        in_specs=in_specs,
        out_specs=pl.BlockSpec((ATTN_TILE, width), lambda b, s, n: (b * per_seq + n, s)),
        out_shape=jax.ShapeDtypeStruct((t, GROUP_WIDTH), BF16),
        scratch_shapes=scratch,
        compiler_params=pltpu.CompilerParams(dimension_semantics=("arbitrary", "arbitrary", "arbitrary"),
                                             vmem_limit_bytes=_vmem_limit(est)),
        name="attn",
    )(*operands, bias)


def _mixout_kernel(x_ref, mod_ref, mix_ref, yc_ref, gate_ref, wao_ref, wco_ref, wo_ref, out_ref, *, d):
    _, _, gate = _mod_rows(mod_ref, 1)
    y_attn = jnp.dot(mix_ref[...], wao_ref[...], preferred_element_type=F32)
    y_conv = jnp.dot(yc_ref[...], wco_ref[...], preferred_element_type=F32)
    g_conv = gate_ref[:, 0:d].astype(F32)
    g_attn = gate_ref[:, d:2 * d].astype(F32)
    merged = _sigmoid(g_conv) * y_conv + _sigmoid(g_attn) * y_attn
    y = jnp.dot(merged.astype(BF16), wo_ref[...], preferred_element_type=F32)
    out_ref[...] = x_ref[...] + gate * y


def _mixout(x, mod, layer, mix, yc, gates, wao, wco, wo, seq, cast=()):
    t, d = x.shape
    tm = FFN_TILE
    per_seq = seq // tm
    row = lambda w: pl.BlockSpec((tm, w), lambda i: (i, 0))
    est = (2 * d * d + GROUP_WIDTH * d) * 2 + 2 * tm * (2 * d * 4 + GROUP_WIDTH * 2 + 3 * d * 2) + tm * d * 4 * 6
    c_in, c_out, c_shapes, c_ops = _casts(cast, t // tm)
    out, *cast_out = pl.pallas_call(
        _with_casts(functools.partial(_mixout_kernel, d=d), 8, 1, len(cast)),
        grid=(t // tm,),
        in_specs=[row(d), pl.BlockSpec((1, 1, N_SUB * 3, d), lambda i: (layer, i // per_seq, 0, 0)),
                  row(GROUP_WIDTH), row(d), row(2 * d),
                  _resident((GROUP_WIDTH, d)), _resident((d, d)), _resident((d, d))] + c_in,
        out_specs=[row(d)] + c_out,
        out_shape=[jax.ShapeDtypeStruct((t, d), F32)] + c_shapes,
        compiler_params=pltpu.CompilerParams(dimension_semantics=("arbitrary",),
                                             vmem_limit_bytes=_vmem_limit(est)),
        name="mixout",
    )(x, mod, mix, yc, gates, wao, wco, wo, *c_ops)
    return out, cast_out


def kernel(x, c, ada_w, ada_b, norm_g, ffn_w_gate, ffn_w_up, ffn_w_down, w_in, conv_w, w_conv_out, w_attn_out,
           w_o, rel_bias, final_g):
    batch, seq, d = x.shape
    depth = ada_w.shape[0]
    assert seq % ATTN_TILE == 0 and ATTN_TILE % TOKEN_TILE == 0 and seq % FFN_TILE == 0
    assert all(win // dil == BLOCK and TOKEN_TILE % (dil * 32) == 0 or dil == 1 for win, dil in DILATION_GROUPS)
    assert w_in.shape[-1] == 3 * QKV_WIDTH + 5 * d and ada_w.shape[-1] == N_SUB * 3 * d

    mod = _ada_mod(c, ada_w, ada_b)
    bias = _t5_bias(rel_bias)
    xt = x.reshape(batch * seq, d)
    fg = final_g.reshape(1, d)
    ffn_w = [w[0, 0].astype(BF16) for w in (ffn_w_gate, ffn_w_up, ffn_w_down)]
    for l in range(depth):
        g = [norm_g[l, j].reshape(1, d) for j in range(N_SUB)]
        xt, (win,) = _ffn(xt, mod, l, 0, g[0], fg, *ffn_w, seq, cast=[(w_in, (l,))])
        (*qkvs, yc, gates), (wao, wco, wo) = _inproj(
            xt, mod, l, g[1], win, conv_w, seq, cast=[(w_attn_out, (l,)), (w_conv_out, (l,)), (w_o, (l,))])
        mix = _attention(qkvs, bias, batch, seq)
        xt, ffn_w = _mixout(xt, mod, l, mix, yc, gates, wao, wco, wo, seq,
                            cast=[(w, (l, 1)) for w in (ffn_w_gate, ffn_w_up, ffn_w_down)])
        nxt = [(w, (l + 1, 0)) for w in (ffn_w_gate, ffn_w_up, ffn_w_down)] if l + 1 < depth else []
        xt, ffn_w = _ffn(xt, mod, l, 2, g[2], fg, *ffn_w, seq, final=l == depth - 1, cast=nxt)
    return xt.reshape(batch, seq, d)
```

```python
import functools

import numpy as np
import jax
import jax.numpy as jnp
from jax import lax
from jax.experimental import pallas as pl
from jax.experimental.pallas import tpu as pltpu

F32 = jnp.float32
BF16 = jnp.bfloat16

HEAD_DIM = 64
HEADS_PER_GROUP = 8
DILATION_GROUPS = ((128, 1), (512, 4), (2048, 16))
N_GROUPS = len(DILATION_GROUPS)
GROUP_WIDTH = HEADS_PER_GROUP * HEAD_DIM
QKV_WIDTH = N_GROUPS * GROUP_WIDTH
NUM_BUCKETS = 32
MAX_DISTANCE = 2048
BLOCK = 128
CONV_K = 3
N_SUB = 3
EPS = 1e-6
LOG2E = float(np.log2(np.e))
LN2 = float(np.log(2.0))

V7X_VMEM_BYTES = 64 * 1024 * 1024
LANES = 128
MXU_TILE = 256
F32_SUBLANES = 8
BF16_SUBLANES = 16

FFN_TILE = 1024
FFN_CHUNKS = 2
TOKEN_TILE = 512
ADA_COL_TILE = 1152
MAX_DILATION = max(d for _, d in DILATION_GROUPS)
ATTN_TILE = BLOCK * MAX_DILATION
SLABS = GROUP_WIDTH // LANES
ATTN_SLABS = 2
G3_PITCH = 24
DEINTERLEAVE_STEP = 4


def _vmem_limit(nbytes):
    return int(min(nbytes * 5 // 4 + (4 << 20), V7X_VMEM_BYTES - (6 << 20)))


def _sigmoid(x):
    return 1.0 / (1.0 + jnp.exp(-x))


def _norm_mod(x, g, shift, scale):
    ms = jnp.mean(x * x, axis=-1, keepdims=True)
    y = x * lax.rsqrt(ms + EPS) * g
    return y * (1.0 + scale) + shift


def _mod_rows(mod_ref, sub):
    base = 3 * sub
    return (mod_ref[0, 0, base:base + 1, :], mod_ref[0, 0, base + 1:base + 2, :],
            mod_ref[0, 0, base + 2:base + 3, :])


def _resident(shape, lead=()):
    index = tuple(lead) + (0,) * len(shape)
    return pl.BlockSpec((None,) * len(lead) + tuple(shape), lambda *_: index, pipeline_mode=pl.Buffered(1))


def _cast_band(rows, steps):
    band = BF16_SUBLANES
    while rows % band or rows // band > steps:
        band += BF16_SUBLANES
    return band


def _casts(sources, steps):
    in_specs, out_specs, out_shapes, operands = [], [], [], []
    for arr, lead in sources:
        rows, cols = arr.shape[-2:]
        band = _cast_band(rows, steps)
        last = rows // band - 1
        in_specs.append(pl.BlockSpec((None,) * len(lead) + (band, cols),
                                     lambda i, lead=lead, last=last: (*lead, jnp.minimum(i, last), 0)))
        out_specs.append(pl.BlockSpec((band, cols), lambda i, last=last: (jnp.minimum(i, last), 0)))
        out_shapes.append(jax.ShapeDtypeStruct((rows, cols), BF16))
        operands.append(arr)
    return in_specs, out_specs, out_shapes, operands


def _with_casts(body, n_in, n_out, n_cast):
    def kernel(*refs):
        ins, rest = refs[:n_in], refs[n_in:]
        cast_in, rest = rest[:n_cast], rest[n_cast:]
        outs, rest = rest[:n_out], rest[n_out:]
        cast_out, scratch = rest[:n_cast], rest[n_cast:]
        for src, dst in zip(cast_in, cast_out):
            dst[...] = src[...].astype(BF16)
        body(*ins, *outs, *scratch)
    return kernel


def _ada_kernel(c_ref, w_ref, b_ref, o_ref):
    c = c_ref[...]
    cs = (c * _sigmoid(c)).astype(BF16)
    o_ref[0] = jnp.dot(cs, w_ref[0].astype(BF16), preferred_element_type=F32) + b_ref[0]


def _ada_mod(c, ada_w, ada_b):
    depth, d, n = ada_w.shape
    b = c.shape[0]
    tn = ADA_COL_TILE
    out = pl.pallas_call(
        _ada_kernel,
        grid=(depth, n // tn),
        in_specs=[pl.BlockSpec((b, d), lambda l, j: (0, 0)),
                  pl.BlockSpec((1, d, tn), lambda l, j: (l, 0, j)),
                  pl.BlockSpec((1, 1, tn), lambda l, j: (l, 0, j))],
        out_specs=pl.BlockSpec((1, b, tn), lambda l, j: (l, 0, j)),
        out_shape=jax.ShapeDtypeStruct((depth, b, n), F32),
        compiler_params=pltpu.CompilerParams(
            dimension_semantics=("arbitrary", "arbitrary"),
            vmem_limit_bytes=_vmem_limit(2 * d * tn * 4 + d * tn * 2)),
        name="ada_mod",
    )(c, ada_w, ada_b.reshape(depth, 1, n))
    return out.reshape(depth, b, N_SUB * 3, d)


def _t5_bucket(dist):
    exact = NUM_BUCKETS // 2
    d = np.maximum(dist, 1).astype(np.float32)
    large = exact + (np.log(d / exact) / np.log(MAX_DISTANCE / exact) * (NUM_BUCKETS - exact)).astype(np.int32)
    large = np.minimum(large, NUM_BUCKETS - 1)
    return np.where(dist < exact, dist, large).astype(np.int32)


def _bucket_maps():
    i = np.arange(BLOCK)[:, None]
    j = np.arange(2 * BLOCK)[None, :]
    rel = np.maximum(i - j + BLOCK, 0)
    return np.stack([_t5_bucket(rel * dil) for _, dil in DILATION_GROUPS]).astype(np.int32)


def _band_valid():
    i = np.arange(BLOCK)[:, None]
    j = np.arange(2 * BLOCK)[None, :]
    in_band = j <= i + BLOCK
    return np.stack([(j >= i) & in_band, (j >= BLOCK) & in_band]).astype(np.int32)


def _bias_kernel(tab_ref, bucket_ref, valid_ref, o_ref):
    g = pl.program_id(0)
    bk = bucket_ref[0]
    for h in range(HEADS_PER_GROUP):
        acc = jnp.zeros(bk.shape, F32)
        for b in range(NUM_BUCKETS):
            acc = jnp.where(bk == b, tab_ref[b, g * HEADS_PER_GROUP + h] * LOG2E, acc)
        for kind in range(2):
            o_ref[0, h, kind] = jnp.where(valid_ref[kind] != 0, acc, -jnp.inf)


def _t5_bias(rel_bias):
    buckets = jnp.asarray(_bucket_maps())
    shape = (N_GROUPS, HEADS_PER_GROUP, 2, BLOCK, 2 * BLOCK)
    return pl.pallas_call(
        _bias_kernel,
        grid=(N_GROUPS,),
        in_specs=[pl.BlockSpec(memory_space=pltpu.SMEM),
                  pl.BlockSpec((1, BLOCK, 2 * BLOCK), lambda g: (g, 0, 0)),
                  pl.BlockSpec((2, BLOCK, 2 * BLOCK), lambda g: (0, 0, 0))],
        out_specs=pl.BlockSpec((1,) + shape[1:], lambda g: (g, 0, 0, 0, 0)),
        out_shape=jax.ShapeDtypeStruct(shape, F32),
        compiler_params=pltpu.CompilerParams(dimension_semantics=("arbitrary",)),
        name="t5_bias",
    )(rel_bias, buckets, jnp.asarray(_band_valid()))


def _ff_chunks(dff):
    tiles = dff // MXU_TILE
    assert tiles * MXU_TILE == dff
    sizes = [tiles // FFN_CHUNKS + (1 if c < tiles % FFN_CHUNKS else 0) for c in range(FFN_CHUNKS)]
    edges = [0]
    for sz in sizes:
        edges.append(edges[-1] + sz * MXU_TILE)
    return list(zip(edges[:-1], edges[1:]))


def _ffn_kernel(x_ref, mod_ref, g_ref, fg_ref, wg_ref, wu_ref, wd_ref, o_ref, *, sub, final):
    x = x_ref[...]
    shift, scale, gate = _mod_rows(mod_ref, sub)
    h = _norm_mod(x, g_ref[...], shift, scale).astype(BF16)
    y = None
    for lo, hi in _ff_chunks(wg_ref.shape[1]):
        a = jnp.dot(h, wg_ref[:, lo:hi], preferred_element_type=F32)
        u = jnp.dot(h, wu_ref[:, lo:hi], preferred_element_type=F32)
        act = (a * _sigmoid(a) * u).astype(BF16)
        part = jnp.dot(act, wd_ref[lo:hi, :], preferred_element_type=F32)
        y = part if y is None else y + part
    out = x + (0.5 * gate) * y
    if final:
        ms = jnp.mean(out * out, axis=-1, keepdims=True)
        out = out * lax.rsqrt(ms + EPS) * fg_ref[...]
    o_ref[...] = out


def _ffn(x, mod, layer, sub, g, final_g, wg, wu, wd, seq, final=False, cast=()):
    t, d = x.shape
    dff = wg.shape[-1]
    tm = FFN_TILE
    per_seq = seq // tm
    steps = t // tm
    c_in, c_out, c_shapes, c_ops = _casts(cast, steps)
    est = 3 * d * dff * 2 + 4 * tm * d * 4 + tm * (dff // FFN_CHUNKS + MXU_TILE) * 10 + tm * d * 10 + (6 << 20)
    body = functools.partial(_ffn_kernel, sub=sub, final=final)
    out, *cast_out = pl.pallas_call(
        _with_casts(body, 7, 1, len(cast)),
        grid=(steps,),
        in_specs=[pl.BlockSpec((tm, d), lambda i: (i, 0)),
                  pl.BlockSpec((1, 1, N_SUB * 3, d), lambda i: (layer, i // per_seq, 0, 0)),
                  _resident((1, d)), _resident((1, d)),
                  _resident((d, dff)), _resident((d, dff)), _resident((dff, d))] + c_in,
        out_specs=[pl.BlockSpec((tm, d), lambda i: (i, 0))] + c_out,
        out_shape=[jax.ShapeDtypeStruct((t, d), F32)] + c_shapes,
        compiler_params=pltpu.CompilerParams(dimension_semantics=("arbitrary",),
                                             vmem_limit_bytes=_vmem_limit(est)),
        name="ffn_final" if final else "ffn",
    )(x, mod, g, final_g, wg, wu, wd, *c_ops)
    return out, cast_out


def _pack_bf16_pair(a, b):
    bits = lambda x: lax.bitcast_convert_type(x.astype(BF16).astype(F32), jnp.uint32)
    return (bits(a) >> 16) | bits(b)


def _unpack_bf16_pair(words):
    low = lax.bitcast_convert_type(words << 16, F32)
    high = lax.bitcast_convert_type(words & jnp.uint32(0xFFFF0000), F32)
    return low.astype(BF16), high.astype(BF16)


def _residue(grun, groups, r):
    low, weight, g = 0, 1, groups
    while g > 1:
        g //= DEINTERLEAVE_STEP
        low += ((grun // g) % DEINTERLEAVE_STEP) * weight
        weight *= DEINTERLEAVE_STEP
    return low + r * weight


def _inproj_kernel(x_ref, mod_ref, g_ref, w_ref, cw_ref, qkv1_ref, qkv2_ref, qkv3_ref, yc_ref,
                   gate_ref, pbuf, ubuf, tbuf, *, per_seq, d):
    tm = x_ref.shape[0]

    @pl.when(pl.program_id(0) % per_seq == 0)
    def _():
        pbuf[0:F32_SUBLANES, :] = jnp.zeros((F32_SUBLANES, d), F32)

    shift, scale, _ = _mod_rows(mod_ref, 1)
    h = _norm_mod(x_ref[...], g_ref[...], shift, scale).astype(BF16)

    def proj(lo, width):
        return jnp.dot(h, w_ref[:, lo:lo + width], preferred_element_type=F32)

    base = 3 * QKV_WIDTH
    p = proj(base + d, d) * proj(base + 2 * d, d)
    pbuf[F32_SUBLANES:F32_SUBLANES + tm, :] = p
    conv = (cw_ref[0:1, :] * pbuf[F32_SUBLANES - 2:F32_SUBLANES - 2 + tm, :]
            + cw_ref[1:2, :] * pbuf[F32_SUBLANES - 1:F32_SUBLANES - 1 + tm, :]
            + cw_ref[2:3, :] * p)
    yc_ref[...] = (proj(base, d) * conv).astype(BF16)
    pbuf[0:F32_SUBLANES, :] = pbuf[tm:tm + F32_SUBLANES, :]

    outs = (qkv1_ref, qkv2_ref, qkv3_ref)
    for part in range(3):
        u = proj(part * QKV_WIDTH, QKV_WIDTH)
        if part == 0:
            u = u * (HEAD_DIM ** -0.5 * LOG2E)
        col = part * GROUP_WIDTH
        for grp, (_, dil) in enumerate(DILATION_GROUPS):
            ug = u[:, grp * GROUP_WIDTH:(grp + 1) * GROUP_WIDTH]
            if dil == 1:
                outs[grp][:, col:col + GROUP_WIDTH] = ug.astype(BF16)
                continue
            buf = ubuf.at[grp - 1]
            for s in range(SLABS // 2):
                pair = [ug[:, (2 * s + e) * LANES:(2 * s + e + 1) * LANES] for e in range(2)]
                buf[s] = _pack_bf16_pair(*pair)
            step, groups = 1, 1
            while step * DEINTERLEAVE_STEP < dil:
                assert buf is not tbuf, "one intermediate buffer: at most two passes"
                run = tm // (groups * DEINTERLEAVE_STEP)
                for s in range(SLABS // 2):
                    for grun in range(groups):
                        for r in range(DEINTERLEAVE_STEP):
                            dst = (grun * DEINTERLEAVE_STEP + r) * run
                            tbuf[s, dst:dst + run, :] = (
                                buf[s, pl.ds(grun * (tm // groups) + r, run, stride=DEINTERLEAVE_STEP), :])
                buf, step, groups = tbuf, step * DEINTERLEAVE_STEP, groups * DEINTERLEAVE_STEP
            last = dil // step
            run = tm // dil
            for s in range(SLABS // 2):
                for grun in range(groups):
                    for r in range(last):
                        res = _residue(grun, groups, r)
                        words = buf[s, pl.ds(grun * (tm // groups) + r, run, stride=last), :]
                        for e, half in enumerate(_unpack_bf16_pair(words)):
                            lo = col + (2 * s + e) * LANES
                            outs[grp][res * run:(res + 1) * run, lo:lo + LANES] = half

    for c in range(2):
        gate_ref[:, c * d:(c + 1) * d] = proj(base + 3 * d + c * d, d).astype(BF16)


def _inproj(x, mod, layer, g, w_in, conv_w, seq, cast=()):
    t, d = x.shape
    n = w_in.shape[-1]
    tm = TOKEN_TILE
    per_seq = seq // tm
    est = (d * n * 2 + 2 * tm * d * 4 + 2 * tm * (3 * QKV_WIDTH + 3 * d) * 2 + (tm + 8) * d * 4
           + N_GROUPS * SLABS * tm * LANES * 4 + tm * d * 4 * 8)
    row = lambda w: pl.BlockSpec((tm, w), lambda i: (i, 0))
    c_in, c_out, c_shapes, c_ops = _casts(cast, t // tm)
    body = functools.partial(_inproj_kernel, per_seq=per_seq, d=d)
    *outs, = pl.pallas_call(
        _with_casts(body, 5, 5, len(cast)),
        grid=(t // tm,),
        in_specs=[row(d),
                  pl.BlockSpec((1, 1, N_SUB * 3, d), lambda i: (layer, i // per_seq, 0, 0)),
                  _resident((1, d)), _resident((d, n)), _resident((CONV_K, d), (layer,))] + c_in,
        out_specs=[row(QKV_WIDTH), row(QKV_WIDTH), row(QKV_WIDTH), row(d), row(2 * d)] + c_out,
        out_shape=[jax.ShapeDtypeStruct((t, QKV_WIDTH), BF16)] * 3
                  + [jax.ShapeDtypeStruct((t, d), BF16), jax.ShapeDtypeStruct((t, 2 * d), BF16)] + c_shapes,
        scratch_shapes=[pltpu.VMEM((tm + F32_SUBLANES, d), F32),
                        pltpu.VMEM((N_GROUPS - 1, SLABS // 2, tm, LANES), jnp.uint32),
                        pltpu.VMEM((SLABS // 2, tm, LANES), jnp.uint32)],
        compiler_params=pltpu.CompilerParams(dimension_semantics=("arbitrary",),
                                             vmem_limit_bytes=_vmem_limit(est)),
        name="inproj",
    )(x, mod, g, w_in, conv_w, *c_ops)
    return outs[:5], outs[5:]


def _attn_kernel(q1, k1, v1, q2, k2, v2, q3, k3, v3, bias_ref, out_ref,
                 pk1, pv1, pk2, pv2, pk3, pv3, o2, l2, o3, l3):
    n = pl.program_id(2)
    lane = lax.broadcasted_iota(jnp.int32, (1, LANES), 1)
    low = lane < HEAD_DIM
    keeps = (low.astype(BF16), 1 - low.astype(BF16))
    nt = (((1,), (1,)), ((), ()))

    def unit(grp, pair, q, k, v, kind):
        v_ext = jnp.concatenate([v, jnp.ones((2 * BLOCK, LANES), BF16)], axis=1)
        parts = []
        for half in range(2):
            s = lax.dot_general(q * keeps[half], k, nt, preferred_element_type=F32)
            s = s + bias_ref[grp, 2 * pair + half, kind]
            m = jnp.max(s, axis=-1, keepdims=True)
            p = jnp.exp2(s - m).astype(BF16)
            oe = jnp.dot(p, v_ext, preferred_element_type=F32)
            parts.append((oe[:, :LANES], oe[:, LANES:], m))
        (oa, la, ma), (ob, lb, mb) = parts
        l = jnp.where(low, la, lb)
        return jnp.where(low, oa, ob) / l, jnp.where(low, ma, mb) * LN2 + jnp.log(l)

    def cat(a, b):
        return jnp.concatenate([a, b], axis=0)

    first = n == 0
    tail_kind = first.astype(jnp.int32)
    full_kind = 0

    @pl.when(first)
    def _():
        for ref in (pk1, pv1, pk2, pv2, pk3, pv3):
            ref[...] = jnp.zeros(ref.shape, BF16)

    d2 = DILATION_GROUPS[1][1]
    d3 = DILATION_GROUPS[2][1]
    run3 = TOKEN_TILE // d3

    for pair in range(ATTN_SLABS):
        ls = slice(pair * LANES, (pair + 1) * LANES)

        def gather3(ref, r):
            return jnp.concatenate([ref[j * TOKEN_TILE + r * run3:j * TOKEN_TILE + (r + 1) * run3, ls]
                                    for j in range(ATTN_TILE // TOKEN_TILE)], axis=0)

        for r in range(d3):
            o, lse = unit(2, pair, gather3(q3, r), cat(gather3(pk3, r), gather3(k3, r)),
                          cat(gather3(pv3, r), gather3(v3, r)), tail_kind)
            o3[pair, pl.ds(r, BLOCK, stride=G3_PITCH), :] = o
            l3[pair, pl.ds(r, BLOCK, stride=G3_PITCH), :] = lse

        for n2 in range(ATTN_TILE // TOKEN_TILE):
            for r in range(d2):
                cur = slice(n2 * TOKEN_TILE + r * BLOCK, n2 * TOKEN_TILE + (r + 1) * BLOCK)
                if n2 == 0:
                    run = slice(r * BLOCK, (r + 1) * BLOCK)
                    k, v, kind = cat(pk2[run, ls], k2[cur, ls]), cat(pv2[run, ls], v2[cur, ls]), tail_kind
                else:
                    prev = slice(cur.start - TOKEN_TILE, cur.stop - TOKEN_TILE)
                    k, v, kind = cat(k2[prev, ls], k2[cur, ls]), cat(v2[prev, ls], v2[cur, ls]), full_kind
                o, lse = unit(1, pair, q2[cur, ls], k, v, kind)
                o2[pair, pl.ds(n2 * TOKEN_TILE + r, BLOCK, stride=d2), :] = o
                l2[pair, pl.ds(n2 * TOKEN_TILE + r, BLOCK, stride=d2), :] = lse

        for u in range(ATTN_TILE // BLOCK):
            cur = slice(u * BLOCK, (u + 1) * BLOCK)
            if u == 0:
                k, v, kind = cat(pk1[:, ls], k1[cur, ls]), cat(pv1[:, ls], v1[cur, ls]), tail_kind
            else:
                both = slice((u - 1) * BLOCK, (u + 1) * BLOCK)
                k, v, kind = k1[both, ls], v1[both, ls], full_kind
            oa, la = unit(0, pair, q1[cur, ls], k, v, kind)
            per = BLOCK // d3

            def rows3(ref):
                return jnp.concatenate([ref[pair, (u * per + i) * G3_PITCH:(u * per + i) * G3_PITCH + d3, :]
                                        for i in range(per)], axis=0)

            lb, lc = l2[pair, cur, :], rows3(l3)
            m = jnp.maximum(jnp.maximum(la, lb), lc)
            ea, eb, ec = jnp.exp(la - m), jnp.exp(lb - m), jnp.exp(lc - m)
            mix = (ea * oa + eb * o2[pair, cur, :] + ec * rows3(o3)) / (ea + eb + ec)
            out_ref[cur, ls] = mix.astype(BF16)

    @pl.when(first)
    def _():
        pk1[...] = k1[ATTN_TILE - BLOCK:, :]
        pv1[...] = v1[ATTN_TILE - BLOCK:, :]
        pk2[...] = k2[ATTN_TILE - TOKEN_TILE:, :]
        pv2[...] = v2[ATTN_TILE - TOKEN_TILE:, :]
        pk3[...] = k3[...]
        pv3[...] = v3[...]


def _attention(qkvs, bias, batch, seq):
    t = qkvs[0].shape[0]
    assert seq == 2 * ATTN_TILE, "the previous-tile carry assumes two attention tiles per sequence"
    per_seq = seq // ATTN_TILE
    width = ATTN_SLABS * LANES
    steps = SLABS // ATTN_SLABS

    def part(p):
        return pl.BlockSpec((ATTN_TILE, width), lambda b, s, n: (b * per_seq + n, p * steps + s))

    in_specs = [part(p) for _ in range(N_GROUPS) for p in range(3)]
    in_specs.append(pl.BlockSpec((N_GROUPS, 2 * ATTN_SLABS, 2, BLOCK, 2 * BLOCK), lambda b, s, n: (0, s, 0, 0, 0)))
    tails = [BLOCK, BLOCK, TOKEN_TILE, TOKEN_TILE, ATTN_TILE, ATTN_TILE]
    scratch = [pltpu.VMEM((rows, width), BF16) for rows in tails]
    scratch += [pltpu.VMEM((ATTN_SLABS, ATTN_TILE, LANES), F32)] * 2
    scratch += [pltpu.VMEM((ATTN_SLABS, BLOCK * G3_PITCH, LANES), F32)] * 2
    est = (2 * 9 * ATTN_TILE * width * 2 + 2 * N_GROUPS * 4 * ATTN_SLABS * BLOCK * 2 * BLOCK * 4
           + sum(tails) * width * 2 + 2 * (N_GROUPS - 1) * ATTN_SLABS * ATTN_TILE * LANES * 4
           + 8 * ATTN_TILE * LANES * 4)
    operands = [a for a in qkvs for _ in range(3)]
    return pl.pallas_call(
        _attn_kernel,
        grid=(batch, steps, per_seq),
        in_specs=in_specs,
        out_specs=pl.BlockSpec((ATTN_TILE, width), lambda b, s, n: (b * per_seq + n, s)),
        out_shape=jax.ShapeDtypeStruct((t, GROUP_WIDTH), BF16),
        scratch_shapes=scratch,
        compiler_params=pltpu.CompilerParams(dimension_semantics=("arbitrary", "arbitrary", "arbitrary"),
                                             vmem_limit_bytes=_vmem_limit(est)),
        name="attn",
    )(*operands, bias)


def _mixout_kernel(x_ref, mod_ref, mix_ref, yc_ref, gate_ref, wao_ref, wco_ref, wo_ref, out_ref, *, d):
    _, _, gate = _mod_rows(mod_ref, 1)
    y_attn = jnp.dot(mix_ref[...], wao_ref[...], preferred_element_type=F32)
    y_conv = jnp.dot(yc_ref[...], wco_ref[...], preferred_element_type=F32)
    g_conv = gate_ref[:, 0:d].astype(F32)
    g_attn = gate_ref[:, d:2 * d].astype(F32)
    merged = _sigmoid(g_conv) * y_conv + _sigmoid(g_attn) * y_attn
    y = jnp.dot(merged.astype(BF16), wo_ref[...], preferred_element_type=F32)
    out_ref[...] = x_ref[...] + gate * y


def _mixout(x, mod, layer, mix, yc, gates, wao, wco, wo, seq, cast=()):
    t, d = x.shape
    tm = FFN_TILE
    per_seq = seq // tm
    row = lambda w: pl.BlockSpec((tm, w), lambda i: (i, 0))
    est = (2 * d * d + GROUP_WIDTH * d) * 2 + 2 * tm * (2 * d * 4 + GROUP_WIDTH * 2 + 3 * d * 2) + tm * d * 4 * 6
    c_in, c_out, c_shapes, c_ops = _casts(cast, t // tm)
    out, *cast_out = pl.pallas_call(
        _with_casts(functools.partial(_mixout_kernel, d=d), 8, 1, len(cast)),
        grid=(t // tm,),
        in_specs=[row(d), pl.BlockSpec((1, 1, N_SUB * 3, d), lambda i: (layer, i // per_seq, 0, 0)),
                  row(GROUP_WIDTH), row(d), row(2 * d),
                  _resident((GROUP_WIDTH, d)), _resident((d, d)), _resident((d, d))] + c_in,
        out_specs=[row(d)] + c_out,
        out_shape=[jax.ShapeDtypeStruct((t, d), F32)] + c_shapes,
        compiler_params=pltpu.CompilerParams(dimension_semantics=("arbitrary",),
                                             vmem_limit_bytes=_vmem_limit(est)),
        name="mixout",
    )(x, mod, mix, yc, gates, wao, wco, wo, *c_ops)
    return out, cast_out


def kernel(x, c, ada_w, ada_b, norm_g, ffn_w_gate, ffn_w_up, ffn_w_down, w_in, conv_w, w_conv_out, w_attn_out,
           w_o, rel_bias, final_g):
    batch, seq, d = x.shape
    depth = ada_w.shape[0]
    assert seq % ATTN_TILE == 0 and ATTN_TILE % TOKEN_TILE == 0 and seq % FFN_TILE == 0
    assert all(win // dil == BLOCK and TOKEN_TILE % (dil * 32) == 0 or dil == 1 for win, dil in DILATION_GROUPS)
    assert w_in.shape[-1] == 3 * QKV_WIDTH + 5 * d and ada_w.shape[-1] == N_SUB * 3 * d

    mod = _ada_mod(c, ada_w, ada_b)
    bias = _t5_bias(rel_bias)
    xt = x.reshape(batch * seq, d)
    fg = final_g.reshape(1, d)
    ffn_w = [w[0, 0].astype(BF16) for w in (ffn_w_gate, ffn_w_up, ffn_w_down)]
    for l in range(depth):
        g = [norm_g[l, j].reshape(1, d) for j in range(N_SUB)]
        xt, (win,) = _ffn(xt, mod, l, 0, g[0], fg, *ffn_w, seq, cast=[(w_in, (l,))])
        (*qkvs, yc, gates), (wao, wco, wo) = _inproj(
            xt, mod, l, g[1], win, conv_w, seq, cast=[(w_attn_out, (l,)), (w_conv_out, (l,)), (w_o, (l,))])
        mix = _attention(qkvs, bias, batch, seq)
        xt, ffn_w = _mixout(xt, mod, l, mix, yc, gates, wao, wco, wo, seq,
                            cast=[(w, (l, 1)) for w in (ffn_w_gate, ffn_w_up, ffn_w_down)])
        nxt = [(w, (l + 1, 0)) for w in (ffn_w_gate, ffn_w_up, ffn_w_down)] if l + 1 < depth else []
        xt, ffn_w = _ffn(xt, mod, l, 2, g[2], fg, *ffn_w, seq, final=l == depth - 1, cast=nxt)
    return xt.reshape(batch, seq, d)
```

```python
import functools

import numpy as np
import jax
import jax.numpy as jnp
from jax import lax
from jax.experimental import pallas as pl
from jax.experimental.pallas import tpu as pltpu

F32 = jnp.float32
BF16 = jnp.bfloat16

HEAD_DIM = 64
HEADS_PER_GROUP = 8
DILATION_GROUPS = ((128, 1), (512, 4), (2048, 16))
N_GROUPS = len(DILATION_GROUPS)
GROUP_WIDTH = HEADS_PER_GROUP * HEAD_DIM
QKV_WIDTH = N_GROUPS * GROUP_WIDTH
NUM_BUCKETS = 32
MAX_DISTANCE = 2048
BLOCK = 128
CONV_K = 3
N_SUB = 3
EPS = 1e-6
LOG2E = float(np.log2(np.e))
LN2 = float(np.log(2.0))

V7X_VMEM_BYTES = 64 * 1024 * 1024
LANES = 128
MXU_TILE = 256
F32_SUBLANES = 8
BF16_SUBLANES = 16

FFN_TILE = 1024
FFN_CHUNKS = 2
TOKEN_TILE = 512
ADA_COL_TILE = 1152
MAX_DILATION = max(d for _, d in DILATION_GROUPS)
ATTN_TILE = BLOCK * MAX_DILATION
SLABS = GROUP_WIDTH // LANES
ATTN_SLABS = 2
G3_PITCH = 20
DEINTERLEAVE_STEP = 4


def _vmem_limit(nbytes):
    return int(min(nbytes * 5 // 4 + (4 << 20), V7X_VMEM_BYTES - (6 << 20)))


def _sigmoid(x):
    return 1.0 / (1.0 + jnp.exp(-x))


def _norm_mod(x, g, shift, scale):
    ms = jnp.mean(x * x, axis=-1, keepdims=True)
    y = x * lax.rsqrt(ms + EPS) * g
    return y * (1.0 + scale) + shift


def _mod_rows(mod_ref, sub):
    base = 3 * sub
    return (mod_ref[0, 0, base:base + 1, :], mod_ref[0, 0, base + 1:base + 2, :],
            mod_ref[0, 0, base + 2:base + 3, :])


def _resident(shape, lead=()):
    index = tuple(lead) + (0,) * len(shape)
    return pl.BlockSpec((None,) * len(lead) + tuple(shape), lambda *_: index, pipeline_mode=pl.Buffered(1))


def _cast_band(rows, steps):
    band = BF16_SUBLANES
    while rows % band or rows // band > steps:
        band += BF16_SUBLANES
    return band


def _casts(sources, steps):
    in_specs, out_specs, out_shapes, operands = [], [], [], []
    for arr, lead in sources:
        rows, cols = arr.shape[-2:]
        band = _cast_band(rows, steps)
        last = rows // band - 1
        in_specs.append(pl.BlockSpec((None,) * len(lead) + (band, cols),
                                     lambda i, lead=lead, last=last: (*lead, jnp.minimum(i, last), 0)))
        out_specs.append(pl.BlockSpec((band, cols), lambda i, last=last: (jnp.minimum(i, last), 0)))
        out_shapes.append(jax.ShapeDtypeStruct((rows, cols), BF16))
        operands.append(arr)
    return in_specs, out_specs, out_shapes, operands


def _with_casts(body, n_in, n_out, n_cast):
    def kernel(*refs):
        ins, rest = refs[:n_in], refs[n_in:]
        cast_in, rest = rest[:n_cast], rest[n_cast:]
        outs, rest = rest[:n_out], rest[n_out:]
        cast_out, scratch = rest[:n_cast], rest[n_cast:]
        for src, dst in zip(cast_in, cast_out):
            dst[...] = src[...].astype(BF16)
        body(*ins, *outs, *scratch)
    return kernel


def _ada_kernel(c_ref, w_ref, b_ref, o_ref):
    c = c_ref[...]
    cs = (c * _sigmoid(c)).astype(BF16)
    o_ref[0] = jnp.dot(cs, w_ref[0].astype(BF16), preferred_element_type=F32) + b_ref[0]


def _ada_mod(c, ada_w, ada_b):
    depth, d, n = ada_w.shape
    b = c.shape[0]
    tn = ADA_COL_TILE
    out = pl.pallas_call(
        _ada_kernel,
        grid=(depth, n // tn),
        in_specs=[pl.BlockSpec((b, d), lambda l, j: (0, 0)),
                  pl.BlockSpec((1, d, tn), lambda l, j: (l, 0, j)),
                  pl.BlockSpec((1, 1, tn), lambda l, j: (l, 0, j))],
        out_specs=pl.BlockSpec((1, b, tn), lambda l, j: (l, 0, j)),
        out_shape=jax.ShapeDtypeStruct((depth, b, n), F32),
        compiler_params=pltpu.CompilerParams(
            dimension_semantics=("arbitrary", "arbitrary"),
            vmem_limit_bytes=_vmem_limit(2 * d * tn * 4 + d * tn * 2)),
        name="ada_mod",
    )(c, ada_w, ada_b.reshape(depth, 1, n))
    return out.reshape(depth, b, N_SUB * 3, d)


def _t5_bucket(dist):
    exact = NUM_BUCKETS // 2
    d = np.maximum(dist, 1).astype(np.float32)
    large = exact + (np.log(d / exact) / np.log(MAX_DISTANCE / exact) * (NUM_BUCKETS - exact)).astype(np.int32)
    large = np.minimum(large, NUM_BUCKETS - 1)
    return np.where(dist < exact, dist, large).astype(np.int32)


def _bucket_maps():
    i = np.arange(BLOCK)[:, None]
    j = np.arange(2 * BLOCK)[None, :]
    rel = np.maximum(i - j + BLOCK, 0)
    return np.stack([_t5_bucket(rel * dil) for _, dil in DILATION_GROUPS]).astype(np.int32)


def _band_valid():
    i = np.arange(BLOCK)[:, None]
    j = np.arange(2 * BLOCK)[None, :]
    in_band = j <= i + BLOCK
    return np.stack([(j >= i) & in_band, (j >= BLOCK) & in_band]).astype(np.int32)


def _bias_kernel(tab_ref, bucket_ref, valid_ref, o_ref):
    g = pl.program_id(0)
    bk = bucket_ref[0]
    for h in range(HEADS_PER_GROUP):
        acc = jnp.zeros(bk.shape, F32)
        for b in range(NUM_BUCKETS):
            acc = jnp.where(bk == b, tab_ref[b, g * HEADS_PER_GROUP + h] * LOG2E, acc)
        for kind in range(2):
            o_ref[0, h, kind] = jnp.where(valid_ref[kind] != 0, acc, -jnp.inf)


def _t5_bias(rel_bias):
    buckets = jnp.asarray(_bucket_maps())
    shape = (N_GROUPS, HEADS_PER_GROUP, 2, BLOCK, 2 * BLOCK)
    return pl.pallas_call(
        _bias_kernel,
        grid=(N_GROUPS,),
        in_specs=[pl.BlockSpec(memory_space=pltpu.SMEM),
                  pl.BlockSpec((1, BLOCK, 2 * BLOCK), lambda g: (g, 0, 0)),
                  pl.BlockSpec((2, BLOCK, 2 * BLOCK), lambda g: (0, 0, 0))],
        out_specs=pl.BlockSpec((1,) + shape[1:], lambda g: (g, 0, 0, 0, 0)),
        out_shape=jax.ShapeDtypeStruct(shape, F32),
        compiler_params=pltpu.CompilerParams(dimension_semantics=("arbitrary",)),
        name="t5_bias",
    )(rel_bias, buckets, jnp.asarray(_band_valid()))


def _ff_chunks(dff):
    tiles = dff // MXU_TILE
    assert tiles * MXU_TILE == dff
    sizes = [tiles // FFN_CHUNKS + (1 if c < tiles % FFN_CHUNKS else 0) for c in range(FFN_CHUNKS)]
    edges = [0]
    for sz in sizes:
        edges.append(edges[-1] + sz * MXU_TILE)
    return list(zip(edges[:-1], edges[1:]))


def _ffn_kernel(x_ref, mod_ref, g_ref, fg_ref, wg_ref, wu_ref, wd_ref, o_ref, *, sub, final):
    x = x_ref[...]
    shift, scale, gate = _mod_rows(mod_ref, sub)
    h = _norm_mod(x, g_ref[...], shift, scale).astype(BF16)
    y = None
    for lo, hi in _ff_chunks(wg_ref.shape[1]):
        a = jnp.dot(h, wg_ref[:, lo:hi], preferred_element_type=F32)
        u = jnp.dot(h, wu_ref[:, lo:hi], preferred_element_type=F32)
        act = (a * _sigmoid(a) * u).astype(BF16)
        part = jnp.dot(act, wd_ref[lo:hi, :], preferred_element_type=F32)
        y = part if y is None else y + part
    out = x + (0.5 * gate) * y
    if final:
        ms = jnp.mean(out * out, axis=-1, keepdims=True)
        out = out * lax.rsqrt(ms + EPS) * fg_ref[...]
    o_ref[...] = out


def _ffn(x, mod, layer, sub, g, final_g, wg, wu, wd, seq, final=False, cast=()):
    t, d = x.shape
    dff = wg.shape[-1]
    tm = FFN_TILE
    per_seq = seq // tm
    steps = t // tm
    c_in, c_out, c_shapes, c_ops = _casts(cast, steps)
    est = 3 * d * dff * 2 + 4 * tm * d * 4 + tm * (dff // FFN_CHUNKS + MXU_TILE) * 10 + tm * d * 10 + (6 << 20)
    body = functools.partial(_ffn_kernel, sub=sub, final=final)
    out, *cast_out = pl.pallas_call(
        _with_casts(body, 7, 1, len(cast)),
        grid=(steps,),
        in_specs=[pl.BlockSpec((tm, d), lambda i: (i, 0)),
                  pl.BlockSpec((1, 1, N_SUB * 3, d), lambda i: (layer, i // per_seq, 0, 0)),
                  _resident((1, d)), _resident((1, d)),
                  _resident((d, dff)), _resident((d, dff)), _resident((dff, d))] + c_in,
        out_specs=[pl.BlockSpec((tm, d), lambda i: (i, 0))] + c_out,
        out_shape=[jax.ShapeDtypeStruct((t, d), F32)] + c_shapes,
        compiler_params=pltpu.CompilerParams(dimension_semantics=("arbitrary",),
                                             vmem_limit_bytes=_vmem_limit(est)),
        name="ffn_final" if final else "ffn",
    )(x, mod, g, final_g, wg, wu, wd, *c_ops)
    return out, cast_out


def _pack_bf16_pair(a, b):
    bits = lambda x: lax.bitcast_convert_type(x.astype(BF16).astype(F32), jnp.uint32)
    return (bits(a) >> 16) | bits(b)


def _unpack_bf16_pair(words):
    low = lax.bitcast_convert_type(words << 16, F32)
    high = lax.bitcast_convert_type(words & jnp.uint32(0xFFFF0000), F32)
    return low.astype(BF16), high.astype(BF16)


def _residue(grun, groups, r):
    low, weight, g = 0, 1, groups
    while g > 1:
        g //= DEINTERLEAVE_STEP
        low += ((grun // g) % DEINTERLEAVE_STEP) * weight
        weight *= DEINTERLEAVE_STEP
    return low + r * weight


def _inproj_kernel(x_ref, mod_ref, g_ref, w_ref, cw_ref, qkv1_ref, qkv2_ref, qkv3_ref, yc_ref,
                   gate_ref, pbuf, ubuf, tbuf, *, per_seq, d):
    tm = x_ref.shape[0]

    @pl.when(pl.program_id(0) % per_seq == 0)
    def _():
        pbuf[0:F32_SUBLANES, :] = jnp.zeros((F32_SUBLANES, d), F32)

    shift, scale, _ = _mod_rows(mod_ref, 1)
    h = _norm_mod(x_ref[...], g_ref[...], shift, scale).astype(BF16)

    def proj(lo, width):
        return jnp.dot(h, w_ref[:, lo:lo + width], preferred_element_type=F32)

    base = 3 * QKV_WIDTH
    p = proj(base + d, d) * proj(base + 2 * d, d)
    pbuf[F32_SUBLANES:F32_SUBLANES + tm, :] = p
    conv = (cw_ref[0:1, :] * pbuf[F32_SUBLANES - 2:F32_SUBLANES - 2 + tm, :]
            + cw_ref[1:2, :] * pbuf[F32_SUBLANES - 1:F32_SUBLANES - 1 + tm, :]
            + cw_ref[2:3, :] * p)
    yc_ref[...] = (proj(base, d) * conv).astype(BF16)
    pbuf[0:F32_SUBLANES, :] = pbuf[tm:tm + F32_SUBLANES, :]

    outs = (qkv1_ref, qkv2_ref, qkv3_ref)
    for part in range(3):
        u = proj(part * QKV_WIDTH, QKV_WIDTH)
        if part == 0:
            u = u * (HEAD_DIM ** -0.5 * LOG2E)
        col = part * GROUP_WIDTH
        for grp, (_, dil) in enumerate(DILATION_GROUPS):
            ug = u[:, grp * GROUP_WIDTH:(grp + 1) * GROUP_WIDTH]
            if dil == 1:
                outs[grp][:, col:col + GROUP_WIDTH] = ug.astype(BF16)
                continue
            buf = ubuf.at[grp - 1]
            for s in range(SLABS // 2):
                pair = [ug[:, (2 * s + e) * LANES:(2 * s + e + 1) * LANES] for e in range(2)]
                buf[s] = _pack_bf16_pair(*pair)
            step, groups = 1, 1
            while step * DEINTERLEAVE_STEP < dil:
                assert buf is not tbuf, "one intermediate buffer: at most two passes"
                run = tm // (groups * DEINTERLEAVE_STEP)
                for s in range(SLABS // 2):
                    for grun in range(groups):
                        for r in range(DEINTERLEAVE_STEP):
                            dst = (grun * DEINTERLEAVE_STEP + r) * run
                            tbuf[s, dst:dst + run, :] = (
                                buf[s, pl.ds(grun * (tm // groups) + r, run, stride=DEINTERLEAVE_STEP), :])
                buf, step, groups = tbuf, step * DEINTERLEAVE_STEP, groups * DEINTERLEAVE_STEP
            last = dil // step
            run = tm // dil
            for s in range(SLABS // 2):
                for grun in range(groups):
                    for r in range(last):
                        res = _residue(grun, groups, r)
                        words = buf[s, pl.ds(grun * (tm // groups) + r, run, stride=last), :]
                        for e, half in enumerate(_unpack_bf16_pair(words)):
                            lo = col + (2 * s + e) * LANES
                            outs[grp][res * run:(res + 1) * run, lo:lo + LANES] = half

    for c in range(2):
        gate_ref[:, c * d:(c + 1) * d] = proj(base + 3 * d + c * d, d).astype(BF16)


def _inproj(x, mod, layer, g, w_in, conv_w, seq, cast=()):
    t, d = x.shape
    n = w_in.shape[-1]
    tm = TOKEN_TILE
    per_seq = seq // tm
    est = (d * n * 2 + 2 * tm * d * 4 + 2 * tm * (3 * QKV_WIDTH + 3 * d) * 2 + (tm + 8) * d * 4
           + N_GROUPS * SLABS * tm * LANES * 4 + tm * d * 4 * 8)
    row = lambda w: pl.BlockSpec((tm, w), lambda i: (i, 0))
    c_in, c_out, c_shapes, c_ops = _casts(cast, t // tm)
    body = functools.partial(_inproj_kernel, per_seq=per_seq, d=d)
    *outs, = pl.pallas_call(
        _with_casts(body, 5, 5, len(cast)),
        grid=(t // tm,),
        in_specs=[row(d),
                  pl.BlockSpec((1, 1, N_SUB * 3, d), lambda i: (layer, i // per_seq, 0, 0)),
                  _resident((1, d)), _resident((d, n)), _resident((CONV_K, d), (layer,))] + c_in,
        out_specs=[row(QKV_WIDTH), row(QKV_WIDTH), row(QKV_WIDTH), row(d), row(2 * d)] + c_out,
        out_shape=[jax.ShapeDtypeStruct((t, QKV_WIDTH), BF16)] * 3
                  + [jax.ShapeDtypeStruct((t, d), BF16), jax.ShapeDtypeStruct((t, 2 * d), BF16)] + c_shapes,
        scratch_shapes=[pltpu.VMEM((tm + F32_SUBLANES, d), F32),
                        pltpu.VMEM((N_GROUPS - 1, SLABS // 2, tm, LANES), jnp.uint32),
                        pltpu.VMEM((SLABS // 2, tm, LANES), jnp.uint32)],
        compiler_params=pltpu.CompilerParams(dimension_semantics=("arbitrary",),
                                             vmem_limit_bytes=_vmem_limit(est)),
        name="inproj",
    )(x, mod, g, w_in, conv_w, *c_ops)
    return outs[:5], outs[5:]


def _attn_kernel(q1, k1, v1, q2, k2, v2, q3, k3, v3, bias_ref, out_ref,
                 pk1, pv1, pk2, pv2, pk3, pv3, o2, l2, o3, l3):
    n = pl.program_id(2)
    lane = lax.broadcasted_iota(jnp.int32, (1, LANES), 1)
    low = lane < HEAD_DIM
    keeps = (low.astype(BF16), 1 - low.astype(BF16))
    nt = (((1,), (1,)), ((), ()))

    def unit(grp, pair, q, k, v, kind):
        v_ext = jnp.concatenate([v, jnp.ones((2 * BLOCK, LANES), BF16)], axis=1)
        parts = []
        for half in range(2):
            s = lax.dot_general(q * keeps[half], k, nt, preferred_element_type=F32)
            s = s + bias_ref[grp, 2 * pair + half, kind]
            m = jnp.max(s, axis=-1, keepdims=True)
            p = jnp.exp2(s - m).astype(BF16)
            oe = jnp.dot(p, v_ext, preferred_element_type=F32)
            parts.append((oe[:, :LANES], oe[:, LANES:], m))
        (oa, la, ma), (ob, lb, mb) = parts
        l = jnp.where(low, la, lb)
        return jnp.where(low, oa, ob) / l, jnp.where(low, ma, mb) * LN2 + jnp.log(l)

    def cat(a, b):
        return jnp.concatenate([a, b], axis=0)

    first = n == 0
    tail_kind = first.astype(jnp.int32)
    full_kind = 0

    @pl.when(first)
    def _():
        for ref in (pk1, pv1, pk2, pv2, pk3, pv3):
            ref[...] = jnp.zeros(ref.shape, BF16)

    d2 = DILATION_GROUPS[1][1]
    d3 = DILATION_GROUPS[2][1]
    run3 = TOKEN_TILE // d3

    for pair in range(ATTN_SLABS):
        ls = slice(pair * LANES, (pair + 1) * LANES)

        def gather3(ref, r):
            return jnp.concatenate([ref[j * TOKEN_TILE + r * run3:j * TOKEN_TILE + (r + 1) * run3, ls]
                                    for j in range(ATTN_TILE // TOKEN_TILE)], axis=0)

        for r in range(d3):
            o, lse = unit(2, pair, gather3(q3, r), cat(gather3(pk3, r), gather3(k3, r)),
                          cat(gather3(pv3, r), gather3(v3, r)), tail_kind)
            o3[pair, pl.ds(r, BLOCK, stride=G3_PITCH), :] = o
            l3[pair, pl.ds(r, BLOCK, stride=G3_PITCH), :] = lse

        for n2 in range(ATTN_TILE // TOKEN_TILE):
            for r in range(d2):
                cur = slice(n2 * TOKEN_TILE + r * BLOCK, n2 * TOKEN_TILE + (r + 1) * BLOCK)
                if n2 == 0:
                    run = slice(r * BLOCK, (r + 1) * BLOCK)
                    k, v, kind = cat(pk2[run, ls], k2[cur, ls]), cat(pv2[run, ls], v2[cur, ls]), tail_kind
                else:
                    prev = slice(cur.start - TOKEN_TILE, cur.stop - TOKEN_TILE)
                    k, v, kind = cat(k2[prev, ls], k2[cur, ls]), cat(v2[prev, ls], v2[cur, ls]), full_kind
                o, lse = unit(1, pair, q2[cur, ls], k, v, kind)
                o2[pair, pl.ds(n2 * TOKEN_TILE + r, BLOCK, stride=d2), :] = o
                l2[pair, pl.ds(n2 * TOKEN_TILE + r, BLOCK, stride=d2), :] = lse

        for u in range(ATTN_TILE // BLOCK):
            cur = slice(u * BLOCK, (u + 1) * BLOCK)
            if u == 0:
                k, v, kind = cat(pk1[:, ls], k1[cur, ls]), cat(pv1[:, ls], v1[cur, ls]), tail_kind
            else:
                both = slice((u - 1) * BLOCK, (u + 1) * BLOCK)
                k, v, kind = k1[both, ls], v1[both, ls], full_kind
            oa, la = unit(0, pair, q1[cur, ls], k, v, kind)
            per = BLOCK // d3

            def rows3(ref):
                return jnp.concatenate([ref[pair, (u * per + i) * G3_PITCH:(u * per + i) * G3_PITCH + d3, :]
                                        for i in range(per)], axis=0)

            lb, lc = l2[pair, cur, :], rows3(l3)
            m = jnp.maximum(jnp.maximum(la, lb), lc)
            ea, eb, ec = jnp.exp(la - m), jnp.exp(lb - m), jnp.exp(lc - m)
            mix = (ea * oa + eb * o2[pair, cur, :] + ec * rows3(o3)) / (ea + eb + ec)
            out_ref[cur, ls] = mix.astype(BF16)

    @pl.when(first)
    def _():
        pk1[...] = k1[ATTN_TILE - BLOCK:, :]
        pv1[...] = v1[ATTN_TILE - BLOCK:, :]
        pk2[...] = k2[ATTN_TILE - TOKEN_TILE:, :]
        pv2[...] = v2[ATTN_TILE - TOKEN_TILE:, :]
        pk3[...] = k3[...]
        pv3[...] = v3[...]


def _attention(qkvs, bias, batch, seq):
    t = qkvs[0].shape[0]
    assert seq == 2 * ATTN_TILE, "the previous-tile carry assumes two attention tiles per sequence"
    per_seq = seq // ATTN_TILE
    width = ATTN_SLABS * LANES
    steps = SLABS // ATTN_SLABS

    def part(p):
        return pl.BlockSpec((ATTN_TILE, width), lambda b, s, n: (b * per_seq + n, p * steps + s))

    in_specs = [part(p) for _ in range(N_GROUPS) for p in range(3)]
    in_specs.append(pl.BlockSpec((N_GROUPS, 2 * ATTN_SLABS, 2, BLOCK, 2 * BLOCK), lambda b, s, n: (0, s, 0, 0, 0)))
    tails = [BLOCK, BLOCK, TOKEN_TILE, TOKEN_TILE, ATTN_TILE, ATTN_TILE]
    scratch = [pltpu.VMEM((rows, width), BF16) for rows in tails]
    scratch += [pltpu.VMEM((ATTN_SLABS, ATTN_TILE, LANES), F32)] * 2
    scratch += [pltpu.VMEM((ATTN_SLABS, BLOCK * G3_PITCH, LANES), F32)] * 2
    est = (2 * 9 * ATTN_TILE * width * 2 + 2 * N_GROUPS * 4 * ATTN_SLABS * BLOCK * 2 * BLOCK * 4
           + sum(tails) * width * 2 + 2 * (N_GROUPS - 1) * ATTN_SLABS * ATTN_TILE * LANES * 4
           + 8 * ATTN_TILE * LANES * 4)
    operands = [a for a in qkvs for _ in range(3)]
    return pl.pallas_call(
        _attn_kernel,
        grid=(batch, steps, per_seq),
        in_specs=in_specs,
        out_specs=pl.BlockSpec((ATTN_TILE, width), lambda b, s, n: (b * per_seq + n, s)),
        out_shape=jax.ShapeDtypeStruct((t, GROUP_WIDTH), BF16),
        scratch_shapes=scratch,
        compiler_params=pltpu.CompilerParams(dimension_semantics=("arbitrary", "arbitrary", "arbitrary"),
                                             vmem_limit_bytes=_vmem_limit(est)),
        name="attn",
    )(*operands, bias)


def _mixout_kernel(x_ref, mod_ref, mix_ref, yc_ref, gate_ref, wao_ref, wco_ref, wo_ref, out_ref, *, d):
    _, _, gate = _mod_rows(mod_ref, 1)
    y_attn = jnp.dot(mix_ref[...], wao_ref[...], preferred_element_type=F32)
    y_conv = jnp.dot(yc_ref[...], wco_ref[...], preferred_element_type=F32)
    g_conv = gate_ref[:, 0:d].astype(F32)
    g_attn = gate_ref[:, d:2 * d].astype(F32)
    merged = _sigmoid(g_conv) * y_conv + _sigmoid(g_attn) * y_attn
    y = jnp.dot(merged.astype(BF16), wo_ref[...], preferred_element_type=F32)
    out_ref[...] = x_ref[...] + gate * y


def _mixout(x, mod, layer, mix, yc, gates, wao, wco, wo, seq, cast=()):
    t, d = x.shape
    tm = FFN_TILE
    per_seq = seq // tm
    row = lambda w: pl.BlockSpec((tm, w), lambda i: (i, 0))
    est = (2 * d * d + GROUP_WIDTH * d) * 2 + 2 * tm * (2 * d * 4 + GROUP_WIDTH * 2 + 3 * d * 2) + tm * d * 4 * 6
    c_in, c_out, c_shapes, c_ops = _casts(cast, t // tm)
    out, *cast_out = pl.pallas_call(
        _with_casts(functools.partial(_mixout_kernel, d=d), 8, 1, len(cast)),
        grid=(t // tm,),
        in_specs=[row(d), pl.BlockSpec((1, 1, N_SUB * 3, d), lambda i: (layer, i // per_seq, 0, 0)),
                  row(GROUP_WIDTH), row(d), row(2 * d),
                  _resident((GROUP_WIDTH, d)), _resident((d, d)), _resident((d, d))] + c_in,
        out_specs=[row(d)] + c_out,
        out_shape=[jax.ShapeDtypeStruct((t, d), F32)] + c_shapes,
        compiler_params=pltpu.CompilerParams(dimension_semantics=("arbitrary",),
                                             vmem_limit_bytes=_vmem_limit(est)),
        name="mixout",
    )(x, mod, mix, yc, gates, wao, wco, wo, *c_ops)
    return out, cast_out


def kernel(x, c, ada_w, ada_b, norm_g, ffn_w_gate, ffn_w_up, ffn_w_down, w_in, conv_w, w_conv_out, w_attn_out,
           w_o, rel_bias, final_g):
    batch, seq, d = x.shape
    depth = ada_w.shape[0]
    assert seq % ATTN_TILE == 0 and ATTN_TILE % TOKEN_TILE == 0 and seq % FFN_TILE == 0
    assert all(win // dil == BLOCK and TOKEN_TILE % (dil * 32) == 0 or dil == 1 for win, dil in DILATION_GROUPS)
    assert w_in.shape[-1] == 3 * QKV_WIDTH + 5 * d and ada_w.shape[-1] == N_SUB * 3 * d

    mod = _ada_mod(c, ada_w, ada_b)
    bias = _t5_bias(rel_bias)
    xt = x.reshape(batch * seq, d)
    fg = final_g.reshape(1, d)
    ffn_w = [w[0, 0].astype(BF16) for w in (ffn_w_gate, ffn_w_up, ffn_w_down)]
    for l in range(depth):
        g = [norm_g[l, j].reshape(1, d) for j in range(N_SUB)]
        xt, (win,) = _ffn(xt, mod, l, 0, g[0], fg, *ffn_w, seq, cast=[(w_in, (l,))])
        (*qkvs, yc, gates), (wao, wco, wo) = _inproj(
            xt, mod, l, g[1], win, conv_w, seq, cast=[(w_attn_out, (l,)), (w_conv_out, (l,)), (w_o, (l,))])
        mix = _attention(qkvs, bias, batch, seq)
        xt, ffn_w = _mixout(xt, mod, l, mix, yc, gates, wao, wco, wo, seq,
                            cast=[(w, (l, 1)) for w in (ffn_w_gate, ffn_w_up, ffn_w_down)])
        nxt = [(w, (l + 1, 0)) for w in (ffn_w_gate, ffn_w_up, ffn_w_down)] if l + 1 < depth else []
        xt, ffn_w = _ffn(xt, mod, l, 2, g[2], fg, *ffn_w, seq, final=l == depth - 1, cast=nxt)
    return xt.reshape(batch, seq, d)
```
